```python
import jax, jax.numpy as jnp
from jax import lax
import numpy as np

D_MODEL = 1024
BATCH = 4
SEQ = 4096
DEPTH = 4
DEC_BATCH = 128
DEC_SEQ = 1
PAST_LEN = 8192
PAGE_SIZE = 128

HEAD_DIM = 64
N_HEADS = D_MODEL // HEAD_DIM
N_KV_HEADS = 4
GROUP = N_HEADS // N_KV_HEADS
Q_DIM = N_HEADS * HEAD_DIM
KV_DIM = N_KV_HEADS * HEAD_DIM
D_FF = 4 * D_MODEL
N_MIXERS = 3
N_SB_LAYERS = (DEPTH + 2) // N_MIXERS
N_NSA_LAYERS = (DEPTH + 1) // N_MIXERS
N_SWA_LAYERS = DEPTH // N_MIXERS
Q_BLOCK = 128
NSA_BLOCK = 64
NSA_TOPK = 16
NSA_WINDOW = 512
SWA_WINDOW = 128
NSA_GATES = 3
ATTN_SCALE = HEAD_DIM ** -0.5
RMS_EPS = 1e-6
NEG_INF = -1e30
TINY = 1e-30
FORCE_SCORE = float(GROUP + 1)

kernel_name = 'hybrid_sb_nsa_swa_decoder_step'


def rms_norm(x, g):
    xf = x.astype(jnp.float32)
    y = xf * lax.rsqrt(jnp.mean(xf * xf, axis=-1, keepdims=True) + RMS_EPS)
    return (y * g.astype(jnp.float32)).astype(x.dtype)


def alibi_slopes():
    h = jnp.arange(1, N_HEADS + 1, dtype=jnp.float32)
    return jnp.exp2(-8.0 * h / N_HEADS).reshape(N_KV_HEADS, GROUP)


def squared_relu_mlp(h, w_up, w_down):
    return jnp.square(jax.nn.relu(h @ w_up)) @ w_down


def project_qkv(h, w_qkv):
    proj = h @ w_qkv
    lead = h.shape[:2]
    q = proj[..., :Q_DIM].reshape(lead + (N_KV_HEADS, GROUP, HEAD_DIM))
    k = proj[..., Q_DIM:Q_DIM + KV_DIM].reshape(lead + (N_KV_HEADS, HEAD_DIM))
    v = proj[..., Q_DIM + KV_DIM:].reshape(lead + (N_KV_HEADS, HEAD_DIM))
    return q, k, v


def gather_pages(pool, layer, page_table):
    pages = pool[layer, page_table]
    b, n, p = pages.shape[:3]
    return pages.reshape(b, n * p, N_KV_HEADS, HEAD_DIM)


def last_rows(x, n):
    return jnp.pad(x, ((0, 0), (n, 0), (0, 0), (0, 0)))[:, -n:]


def split_query_blocks(q):
    b, s = q.shape[:2]
    return q.reshape(b, s // Q_BLOCK, Q_BLOCK, N_KV_HEADS, GROUP, HEAD_DIM).swapaxes(0, 1)


def merge_query_blocks(o):
    o = o.swapaxes(0, 1)
    return o.reshape((o.shape[0], o.shape[1] * o.shape[2]) + o.shape[3:])


def stick_breaking(q, k, v, q_pos, k_pos):
    z = jnp.einsum('btkgd,bskd->bkgts', q, k).astype(jnp.float32) * ATTN_SCALE
    before = k_pos[None, :] < q_pos[:, None]
    log_1m = jnp.where(before, jax.nn.log_sigmoid(-z), 0.0)
    between = lax.cumsum(log_1m, axis=z.ndim - 1, reverse=True) - log_1m
    w = jnp.where(before, jnp.exp(jax.nn.log_sigmoid(z) + between), 0.0)
    return jnp.einsum('bkgts,bskd->btkgd', w.astype(v.dtype), v)


def sb_prompt(h, w_qkv, w_o):
    b, s, _ = h.shape
    q, k, v = project_qkv(h, w_qkv)
    pos = jnp.arange(s)
    pos_blocks = pos.reshape(s // Q_BLOCK, Q_BLOCK)
    o = lax.map(lambda a: stick_breaking(a[0], k, v, a[1], pos), (split_query_blocks(q), pos_blocks))
    return merge_query_blocks(o).reshape(b, s, Q_DIM) @ w_o, k, v


def sb_sample(h, pool_k, pool_v, layer, page_table, w_qkv, w_o):
    b, t, _ = h.shape
    q, k, v = project_qkv(h, w_qkv)
    past_k = gather_pages(pool_k, layer, page_table)
    past_v = gather_pages(pool_v, layer, page_table)
    past_len = past_k.shape[1]
    k_all = jnp.concatenate([past_k, k], axis=1)
    v_all = jnp.concatenate([past_v, v], axis=1)
    o = stick_breaking(q, k_all, v_all, past_len + jnp.arange(t), jnp.arange(past_len + t))
    return o.reshape(b, t, Q_DIM) @ w_o, k, v


def softmax_attend(q, k, v, q_pos, k_pos, window, slopes, sink):
    s = jnp.einsum('...tkgd,...skd->...kgts', q, k).astype(jnp.float32) * ATTN_SCALE
    dist = q_pos[..., :, None] - k_pos[..., None, :]
    valid = (dist >= 0) & (dist <= window) & (k_pos[..., None, :] >= 0)
    dist = dist[..., None, None, :, :].astype(jnp.float32)
    valid = valid[..., None, None, :, :]
    s = jnp.where(valid, s - slopes[:, :, None, None] * dist, NEG_INF)
    m = jnp.max(s, axis=-1, keepdims=True)
    if sink is not None:
        sink = sink.astype(jnp.float32)[:, :, None, None]
        m = jnp.maximum(m, sink)
    p = jnp.exp(s - m)
    denom = jnp.sum(p, axis=-1, keepdims=True)
    if sink is not None:
        denom = denom + jnp.exp(sink - m)
    return jnp.einsum('...kgts,...skd->...tkgd', (p / denom).astype(v.dtype), v)


def banded_attention(q, k, v, window, slopes, sink):
    b, s = k.shape[:2]
    nqb = s // Q_BLOCK
    n_keys = window + Q_BLOCK
    pad = ((0, 0), (window, 0), (0, 0), (0, 0))
    idx = jnp.arange(nqb)[:, None] * Q_BLOCK + jnp.arange(n_keys)[None, :]
    kb = jnp.pad(k, pad)[:, idx]
    vb = jnp.pad(v, pad)[:, idx]
    qb = q.reshape(b, nqb, Q_BLOCK, N_KV_HEADS, GROUP, HEAD_DIM)
    q_pos = jnp.arange(s).reshape(nqb, Q_BLOCK)
    o = softmax_attend(qb, kb, vb, q_pos, idx - window, window, slopes, sink)
    return o.reshape(b, s, N_KV_HEADS, GROUP, HEAD_DIM)


def nsa_project(h, w_in):
    proj = h @ w_in
    lead = h.shape[:2]
    q = proj[..., :Q_DIM].reshape(lead + (N_KV_HEADS, GROUP, HEAD_DIM))
    kv = proj[..., Q_DIM:Q_DIM + 6 * KV_DIM].reshape(lead + (6, N_KV_HEADS, HEAD_DIM))
    gates = jax.nn.sigmoid(proj[..., Q_DIM + 6 * KV_DIM:].astype(jnp.float32))
    gates = gates.reshape(lead + (N_KV_HEADS, GROUP, NSA_GATES)).astype(h.dtype)
    return q, kv[:, :, 0], kv[:, :, 1], kv[:, :, 2], kv[:, :, 3], kv[:, :, 4], kv[:, :, 5], gates


def pad_to_block(x):
    n = (-x.shape[1]) % NSA_BLOCK
    return jnp.pad(x, ((0, 0), (0, n), (0, 0), (0, 0)))


def compress(x, w):
    b, l = x.shape[:2]
    xb = x.reshape(b, l // NSA_BLOCK, NSA_BLOCK, N_KV_HEADS, HEAD_DIM)
    return jnp.einsum('bnikd,ik->bnkd', xb, w.astype(x.dtype))


def to_blocks(x):
    b, l = x.shape[:2]
    return x.reshape(b, l // NSA_BLOCK, NSA_BLOCK, N_KV_HEADS, HEAD_DIM).transpose(0, 3, 1, 2, 4)


def nsa_compressed(q, kc, vc, q_pos, slopes):
    nb = kc.shape[1]
    s = jnp.einsum('btkgd,bnkd->bkgtn', q, kc).astype(jnp.float32) * ATTN_SCALE
    blk_end = jnp.arange(nb) * NSA_BLOCK + NSA_BLOCK - 1
    dist = q_pos[:, None] - blk_end[None, :]
    valid = dist >= 0
    s = jnp.where(valid, s - slopes[:, :, None, None] * dist.astype(jnp.float32), NEG_INF)
    m = jnp.max(s, axis=-1, keepdims=True)
    p = jnp.where(valid, jnp.exp(s - m), 0.0)
    p = p / jnp.maximum(jnp.sum(p, axis=-1, keepdims=True), TINY)
    o = jnp.einsum('bkgtn,bnkd->btkgd', p.astype(vc.dtype), vc)
    return o, p


def nsa_select(p_cmp, q_pos):
    nb = p_cmp.shape[-1]
    imp = jnp.sum(p_cmp, axis=2)
    blk = jnp.arange(nb)[None, :]
    cur = (q_pos // NSA_BLOCK)[:, None]
    score = jnp.where(blk == cur, FORCE_SCORE, jnp.where(blk < cur, imp, -1.0))
    _, idx = lax.top_k(score, min(NSA_TOPK, nb))
    return idx


def nsa_selected(q, kb, vb, idx, q_pos, slopes):
    bi = jnp.arange(q.shape[0])[:, None, None, None]
    hi = jnp.arange(N_KV_HEADS)[None, :, None, None]
    ksel = kb[bi, hi, idx]
    vsel = vb[bi, hi, idx]
    s = jnp.einsum('btkgd,bktnid->bkgtni', q, ksel).astype(jnp.float32) * ATTN_SCALE
    kpos = idx[..., None] * NSA_BLOCK + jnp.arange(NSA_BLOCK)
    dist = (q_pos[:, None, None] - kpos)[:, :, None]
    s = jnp.where(dist >= 0, s - slopes[:, :, None, None, None] * dist.astype(jnp.float32), NEG_INF)
    p = jax.nn.softmax(s.reshape(s.shape[:4] + (-1,)), axis=-1).reshape(s.shape)
    return jnp.einsum('bkgtni,bktnid->btkgd', p.astype(vsel.dtype), vsel)


def nsa_combine(o_cmp, o_sel, o_win, gates, w_o):
    o = gates[..., 0:1] * o_cmp + gates[..., 1:2] * o_sel + gates[..., 2:3] * o_win
    return o.reshape(o.shape[:2] + (Q_DIM,)) @ w_o


def nsa_prompt(h, w_in, w_o, cmp_wk, cmp_wv, slopes, n_buf):
    b, s, _ = h.shape
    q, kc, vc, ks, vs, kw, vw, gates = nsa_project(h, w_in)
    pos = jnp.arange(s)
    o_cmp, p_cmp = nsa_compressed(q, compress(kc, cmp_wk), compress(vc, cmp_wv), pos, slopes)
    idx = nsa_select(p_cmp, pos)
    ksb, vsb = to_blocks(ks), to_blocks(vs)
    nqb = s // Q_BLOCK
    idx_blocks = idx.reshape(b, N_KV_HEADS, nqb, Q_BLOCK, idx.shape[-1]).transpose(2, 0, 1, 3, 4)
    pos_blocks = pos.reshape(nqb, Q_BLOCK)
    o_sel = lax.map(lambda a: nsa_selected(a[0], ksb, vsb, a[1], a[2], slopes),
                    (split_query_blocks(q), idx_blocks, pos_blocks))
    o_sel = merge_query_blocks(o_sel)
    o_win = banded_attention(q, kw, vw, NSA_WINDOW, slopes, None)
    out = nsa_combine(o_cmp, o_sel, o_win, gates, w_o)
    return out, (kc, vc, ks, vs, last_rows(kw, n_buf), last_rows(vw, n_buf))


def nsa_sample(h, pool_ck, pool_cv, pool_sk, pool_sv, win_k, win_v, layer, page_table,
               w_in, w_o, cmp_wk, cmp_wv, slopes):
    b, t, _ = h.shape
    q, kc, vc, ks, vs, kw, vw, gates = nsa_project(h, w_in)
    past_ck = gather_pages(pool_ck, layer, page_table)
    past_len = past_ck.shape[1]
    q_pos = past_len + jnp.arange(t)
    kc_all = pad_to_block(jnp.concatenate([past_ck, kc], axis=1))
    vc_all = pad_to_block(jnp.concatenate([gather_pages(pool_cv, layer, page_table), vc], axis=1))
    ks_all = pad_to_block(jnp.concatenate([gather_pages(pool_sk, layer, page_table), ks], axis=1))
    vs_all = pad_to_block(jnp.concatenate([gather_pages(pool_sv, layer, page_table), vs], axis=1))
    o_cmp, p_cmp = nsa_compressed(q, compress(kc_all, cmp_wk), compress(vc_all, cmp_wv), q_pos, slopes)
    idx = nsa_select(p_cmp, q_pos)
    o_sel = nsa_selected(q, to_blocks(ks_all), to_blocks(vs_all), idx, q_pos, slopes)
    n_buf = win_k.shape[1]
    kw_all = jnp.concatenate([win_k, kw], axis=1)
    vw_all = jnp.concatenate([win_v, vw], axis=1)
    k_pos = past_len - n_buf + jnp.arange(n_buf + t)
    o_win = softmax_attend(q, kw_all, vw_all, q_pos, k_pos, NSA_WINDOW, slopes, None)
    out = nsa_combine(o_cmp, o_sel, o_win, gates, w_o)
    return out, (kc, vc, ks, vs, kw_all[:, -n_buf:], vw_all[:, -n_buf:])


def swa_prompt(h, w_qkv, w_o, sinks, slopes, n_buf):
    b, s, _ = h.shape
    q, k, v = project_qkv(h, w_qkv)
    o = banded_attention(q, k, v, SWA_WINDOW, slopes, sinks.reshape(N_KV_HEADS, GROUP))
    return o.reshape(b, s, Q_DIM) @ w_o, last_rows(k, n_buf), last_rows(v, n_buf)


def swa_sample(h, buf_k, buf_v, past_len, w_qkv, w_o, sinks, slopes):
    b, t, _ = h.shape
    q, k, v = project_qkv(h, w_qkv)
    n_buf = buf_k.shape[1]
    k_all = jnp.concatenate([buf_k, k], axis=1)
    v_all = jnp.concatenate([buf_v, v], axis=1)
    k_pos = past_len - n_buf + jnp.arange(n_buf + t)
    o = softmax_attend(q, k_all, v_all, past_len + jnp.arange(t), k_pos, SWA_WINDOW, slopes,
                       sinks.reshape(N_KV_HEADS, GROUP))
    return o.reshape(b, t, Q_DIM) @ w_o, k_all[:, -n_buf:], v_all[:, -n_buf:]


def setup_inputs(seed: int = 0) -> dict:
    key = jax.random.key(seed)
    ks = jax.random.split(key, 32)
    f32 = jnp.float32
    n_pages = PAST_LEN // PAGE_SIZE
    n_used = DEC_BATCH * n_pages
    n_pool = n_used + max(1, n_used // 4)
    page_table = jax.random.permutation(ks[0], n_pool)[:n_used].reshape(DEC_BATCH, n_pages).astype(jnp.int32)
    nsa_buf = min(NSA_WINDOW, PAST_LEN)
    swa_buf = min(SWA_WINDOW, PAST_LEN)

    def nrm(k, shape, scale=1.0):
        return scale * jax.random.normal(k, shape, f32)

    def pool(k, n_layers):
        return nrm(k, (n_layers, n_pool, PAGE_SIZE, N_KV_HEADS, HEAD_DIM))

    nsa_in_cols = Q_DIM + 6 * KV_DIM + NSA_GATES * N_HEADS
    return {
        'x_prompt': nrm(ks[1], (BATCH, SEQ, D_MODEL)),
        'x_sample': nrm(ks[2], (DEC_BATCH, DEC_SEQ, D_MODEL)),
        'cache_sb_k': pool(ks[3], N_SB_LAYERS),
        'cache_sb_v': pool(ks[4], N_SB_LAYERS),
        'cache_nsa_cmp_k': pool(ks[5], N_NSA_LAYERS),
        'cache_nsa_cmp_v': pool(ks[6], N_NSA_LAYERS),
        'cache_nsa_sel_k': pool(ks[7], N_NSA_LAYERS),
        'cache_nsa_sel_v': pool(ks[8], N_NSA_LAYERS),
        'cache_nsa_win_k': nrm(ks[9], (N_NSA_LAYERS, DEC_BATCH, nsa_buf, N_KV_HEADS, HEAD_DIM)),
        'cache_nsa_win_v': nrm(ks[10], (N_NSA_LAYERS, DEC_BATCH, nsa_buf, N_KV_HEADS, HEAD_DIM)),
        'cache_swa_k': nrm(ks[11], (N_SWA_LAYERS, DEC_BATCH, swa_buf, N_KV_HEADS, HEAD_DIM)),
        'cache_swa_v': nrm(ks[12], (N_SWA_LAYERS, DEC_BATCH, swa_buf, N_KV_HEADS, HEAD_DIM)),
        'page_table': page_table,
        'norm_mix': 1.0 + nrm(ks[13], (DEPTH, D_MODEL), 0.02),
        'norm_ffn': 1.0 + nrm(ks[14], (DEPTH, D_MODEL), 0.02),
        'norm_final': 1.0 + nrm(ks[15], (D_MODEL,), 0.02),
        'w_sb_qkv': nrm(ks[16], (N_SB_LAYERS, D_MODEL, Q_DIM + 2 * KV_DIM), D_MODEL ** -0.5),
        'w_sb_o': nrm(ks[17], (N_SB_LAYERS, Q_DIM, D_MODEL), Q_DIM ** -0.5),
        'w_nsa_in': nrm(ks[18], (N_NSA_LAYERS, D_MODEL, nsa_in_cols), D_MODEL ** -0.5),
        'w_nsa_o': nrm(ks[19], (N_NSA_LAYERS, Q_DIM, D_MODEL), Q_DIM ** -0.5),
        'nsa_cmp_wk': (1.0 + nrm(ks[20], (N_NSA_LAYERS, NSA_BLOCK, N_KV_HEADS), 0.1)) / NSA_BLOCK,
        'nsa_cmp_wv': (1.0 + nrm(ks[21], (N_NSA_LAYERS, NSA_BLOCK, N_KV_HEADS), 0.1)) / NSA_BLOCK,
        'w_swa_qkv': nrm(ks[22], (N_SWA_LAYERS, D_MODEL, Q_DIM + 2 * KV_DIM), D_MODEL ** -0.5),
        'w_swa_o': nrm(ks[23], (N_SWA_LAYERS, Q_DIM, D_MODEL), Q_DIM ** -0.5),
        'swa_sinks': nrm(ks[24], (N_SWA_LAYERS, N_HEADS)),
        'w_ffn_up': nrm(ks[25], (DEPTH, D_MODEL, D_FF), D_MODEL ** -0.5),
        'w_ffn_down': nrm(ks[26], (DEPTH, D_FF, D_MODEL), D_FF ** -0.5),
    }


def reference(x_prompt, x_sample, cache_sb_k, cache_sb_v, cache_nsa_cmp_k, cache_nsa_cmp_v,
              cache_nsa_sel_k, cache_nsa_sel_v, cache_nsa_win_k, cache_nsa_win_v,
              cache_swa_k, cache_swa_v, page_table, norm_mix, norm_ffn, norm_final,
              w_sb_qkv, w_sb_o, w_nsa_in, w_nsa_o, nsa_cmp_wk, nsa_cmp_wv,
              w_swa_qkv, w_swa_o, swa_sinks, w_ffn_up, w_ffn_down):
    slopes = alibi_slopes()
    past_len = page_table.shape[1] * cache_sb_k.shape[2]
    nsa_buf = cache_nsa_win_k.shape[2]
    swa_buf = cache_swa_k.shape[2]
    xp, xs = x_prompt, x_sample
    sb_p, sb_s, nsa_p, nsa_s, swa_p, swa_s = [], [], [], [], [], []
    ia = ib = ic = 0
    for i in range(DEPTH):
        hp = rms_norm(xp, norm_mix[i])
        hs = rms_norm(xs, norm_mix[i])
        kind = i % N_MIXERS
        if kind == 0:
            o_p, k_p, v_p = sb_prompt(hp, w_sb_qkv[ia], w_sb_o[ia])
            o_s, k_s, v_s = sb_sample(hs, cache_sb_k, cache_sb_v, ia, page_table, w_sb_qkv[ia], w_sb_o[ia])
            sb_p.append((k_p, v_p))
            sb_s.append((k_s, v_s))
            ia += 1
        elif kind == 1:
            o_p, st_p = nsa_prompt(hp, w_nsa_in[ib], w_nsa_o[ib], nsa_cmp_wk[ib], nsa_cmp_wv[ib], slopes, nsa_buf)
            o_s, st_s = nsa_sample(hs, cache_nsa_cmp_k, cache_nsa_cmp_v, cache_nsa_sel_k, cache_nsa_sel_v,
                                   cache_nsa_win_k[ib], cache_nsa_win_v[ib], ib, page_table,
                                   w_nsa_in[ib], w_nsa_o[ib], nsa_cmp_wk[ib], nsa_cmp_wv[ib], slopes)
            nsa_p.append(st_p)
            nsa_s.append(st_s)
            ib += 1
        else:
            o_p, k_p, v_p = swa_prompt(hp, w_swa_qkv[ic], w_swa_o[ic], swa_sinks[ic], slopes, swa_buf)
            o_s, k_s, v_s = swa_sample(hs, cache_swa_k[ic], cache_swa_v[ic], past_len,
                                       w_swa_qkv[ic], w_swa_o[ic], swa_sinks[ic], slopes)
            swa_p.append((k_p, v_p))
            swa_s.append((k_s, v_s))
            ic += 1
        xp = xp + o_p
        xs = xs + o_s
        xp = xp + squared_relu_mlp(rms_norm(xp, norm_ffn[i]), w_ffn_up[i], w_ffn_down[i])
        xs = xs + squared_relu_mlp(rms_norm(xs, norm_ffn[i]), w_ffn_up[i], w_ffn_down[i])
    y_prompt = rms_norm(xp, norm_final)
    y_sample = rms_norm(xs, norm_final)
    sb_k_prompt, sb_v_prompt = [jnp.stack(t) for t in zip(*sb_p)]
    sb_k_sample, sb_v_sample = [jnp.stack(t) for t in zip(*sb_s)]
    (nsa_cmp_k_prompt, nsa_cmp_v_prompt, nsa_sel_k_prompt, nsa_sel_v_prompt,
     nsa_win_k_prompt, nsa_win_v_prompt) = [jnp.stack(t) for t in zip(*nsa_p)]
    (nsa_cmp_k_sample, nsa_cmp_v_sample, nsa_sel_k_sample, nsa_sel_v_sample,
     nsa_win_k_sample, nsa_win_v_sample) = [jnp.stack(t) for t in zip(*nsa_s)]
    swa_k_prompt, swa_v_prompt = [jnp.stack(t) for t in zip(*swa_p)]
    swa_k_sample, swa_v_sample = [jnp.stack(t) for t in zip(*swa_s)]
    return (y_prompt, y_sample,
            sb_k_prompt, sb_v_prompt, sb_k_sample, sb_v_sample,
            nsa_cmp_k_prompt, nsa_cmp_v_prompt, nsa_cmp_k_sample, nsa_cmp_v_sample,
            nsa_sel_k_prompt, nsa_sel_v_prompt, nsa_sel_k_sample, nsa_sel_v_sample,
            nsa_win_k_prompt, nsa_win_v_prompt, nsa_win_k_sample, nsa_win_v_sample,
            swa_k_prompt, swa_v_prompt, swa_k_sample, swa_v_sample)
```

```python
import functools
import math

import jax
import jax.numpy as jnp
from jax import lax
from jax.experimental import pallas as pl
from jax.experimental.pallas import tpu as pltpu

F32 = jnp.float32
BF16 = jnp.bfloat16
I32 = jnp.int32

D_MODEL = 1024
HEAD_DIM = 64
N_HEADS = 16
N_KV_HEADS = 4
GROUP = 4
Q_DIM = N_HEADS * HEAD_DIM
KV_DIM = N_KV_HEADS * HEAD_DIM
N_MIXERS = 3
NSA_BLOCK = 64
NSA_TOPK = 16
NSA_WINDOW = 512
SWA_WINDOW = 128
NSA_GATES = 3
ATTN_SCALE = HEAD_DIM ** -0.5
RMS_EPS = 1e-6
NEG_INF = -1e30
TINY = 1e-30
FORCE_SCORE = float(GROUP + 1)

LANES = 128
TQ = 128
TK = 128
GT = GROUP * TQ
GROUP_SHIFT = GROUP.bit_length() - 1
TQ_SHIFT = TQ.bit_length() - 1
NSA_BLOCK_SHIFT = NSA_BLOCK.bit_length() - 1
PAGES_PER_STEP = 8
SB_DEAD_LOG = -104.0
VMEM_LIMIT = 48 * 1024 * 1024


def _cparams(sem):
    return pltpu.CompilerParams(dimension_semantics=sem, vmem_limit_bytes=VMEM_LIMIT)


def _nt_dot(a, b):
    return lax.dot_general(a, b, (((1,), (1,)), ((), ())), preferred_element_type=F32)


def _dot(a, b):
    return jnp.dot(a, b, preferred_element_type=F32)


def _rms(x, g):
    ms = jnp.mean(x * x, axis=-1, keepdims=True)
    return (x * lax.rsqrt(ms + RMS_EPS)) * g


def _norm_proj_kernel(x_ref, g_ref, w_ref, o_ref):
    h = _rms(x_ref[...], g_ref[...])
    o_ref[...] = _dot(h.astype(BF16), w_ref[...])


def norm_proj(x, g, w, tm):
    m, d = x.shape
    n = w.shape[1]
    return pl.pallas_call(
        _norm_proj_kernel,
        grid=(m // tm,),
        in_specs=[pl.BlockSpec((tm, d), lambda i: (i, 0)),
                  pl.BlockSpec((1, d), lambda i: (0, 0)),
                  pl.BlockSpec((d, n), lambda i: (0, 0))],
        out_specs=pl.BlockSpec((tm, n), lambda i: (i, 0)),
        out_shape=jax.ShapeDtypeStruct((m, n), F32),
        compiler_params=_cparams(("parallel",)),
        name="norm_proj",
    )(x, g.reshape(1, d), w)


def _out_proj_kernel(res_ref, a_ref, w_ref, o_ref):
    o_ref[...] = res_ref[...] + _dot(a_ref[...].astype(BF16), w_ref[...])


def out_proj(res, a, w, tm):
    m, d = res.shape
    k = a.shape[1]
    return pl.pallas_call(
        _out_proj_kernel,
        grid=(m // tm,),
        in_specs=[pl.BlockSpec((tm, d), lambda i: (i, 0)),
                  pl.BlockSpec((tm, k), lambda i: (i, 0)),
                  pl.BlockSpec((k, d), lambda i: (0, 0))],
        out_specs=pl.BlockSpec((tm, d), lambda i: (i, 0)),
        out_shape=jax.ShapeDtypeStruct((m, d), F32),
        compiler_params=_cparams(("parallel",)),
        name="out_proj",
    )(res, a, w)


def _nsa_out_proj_kernel(res_ref, g0_ref, g1_ref, g2_ref, a0_ref, a1_ref, a2_ref, w_ref, o_ref):
    o = (jax.nn.sigmoid(g0_ref[...]) * a0_ref[...]
         + jax.nn.sigmoid(g1_ref[...]) * a1_ref[...]
         + jax.nn.sigmoid(g2_ref[...]) * a2_ref[...])
    o_ref[...] = res_ref[...] + _dot(o.astype(BF16), w_ref[...])


def nsa_out_proj(res, gate_logits, branches, w, tm):
    m, d = res.shape
    row = pl.BlockSpec((tm, d), lambda i: (i, 0))
    return pl.pallas_call(
        _nsa_out_proj_kernel,
        grid=(m // tm,),
        in_specs=[row] * 7 + [pl.BlockSpec((d, d), lambda i: (0, 0))],
        out_specs=row,
        out_shape=jax.ShapeDtypeStruct((m, d), F32),
        compiler_params=_cparams(("parallel",)),
        name="nsa_out_proj",
    )(res, *gate_logits, *branches, w)


def _ffn_kernel(x_ref, g_ref, wu_ref, wd_ref, o_ref, h_sc, acc_sc):
    f = pl.program_id(1)

    @pl.when(f == 0)
    def _():
        h_sc[...] = _rms(x_ref[...], g_ref[...]).astype(BF16)
        acc_sc[...] = jnp.zeros_like(acc_sc)

    u = jnp.maximum(_dot(h_sc[...], wu_ref[...]), 0.0)
    acc_sc[...] += _dot((u * u).astype(BF16), wd_ref[...])

    @pl.when(f == pl.num_programs(1) - 1)
    def _():
        o_ref[...] = x_ref[...] + acc_sc[...]


def ffn(x, g, w_up, w_down, tm, tf):
    m, d = x.shape
    dff = w_up.shape[1]
    return pl.pallas_call(
        _ffn_kernel,
        grid=(m // tm, dff // tf),
        in_specs=[pl.BlockSpec((tm, d), lambda i, f: (i, 0)),
                  pl.BlockSpec((1, d), lambda i, f: (0, 0)),
                  pl.BlockSpec((d, tf), lambda i, f: (0, f)),
                  pl.BlockSpec((tf, d), lambda i, f: (f, 0))],
        out_specs=pl.BlockSpec((tm, d), lambda i, f: (i, 0)),
        out_shape=jax.ShapeDtypeStruct((m, d), F32),
        scratch_shapes=[pltpu.VMEM((tm, d), BF16), pltpu.VMEM((tm, d), F32)],
        compiler_params=_cparams(("parallel", "arbitrary")),
        name="ffn",
    )(x, g.reshape(1, d), w_up, w_down)


def _final_norm_kernel(x_ref, g_ref, o_ref):
    o_ref[...] = _rms(x_ref[...], g_ref[...])


def final_norm(x, g, tm):
    m, d = x.shape
    return pl.pallas_call(
        _final_norm_kernel,
        grid=(m // tm,),
        in_specs=[pl.BlockSpec((tm, d), lambda i: (i, 0)), pl.BlockSpec((1, d), lambda i: (0, 0))],
        out_specs=pl.BlockSpec((tm, d), lambda i: (i, 0)),
        out_shape=jax.ShapeDtypeStruct((m, d), F32),
        compiler_params=_cparams(("parallel",)),
        name="final_norm",
    )(x, g.reshape(1, d))


def _sb_tile(z, uo, cb, before):
    sp = jnp.maximum(z, 0.0) + jnp.log(1.0 + jnp.exp(-jnp.abs(z)))
    lm = -sp
    if before is not None:
        lm = jnp.where(before, lm, 0.0)
    hi = lm.astype(BF16)
    lo = (lm - hi.astype(F32)).astype(BF16)
    r = _dot(hi, uo) + _dot(lo, uo)
    w = jnp.exp((z - sp) + r[:, :TK] + cb)
    if before is not None:
        w = jnp.where(before, w, 0.0)
    return w, cb + r[:, TK:]


def _sb_prompt_kernel(q_ref, k_ref, v_ref, uo_ref, o_ref, acc_sc, cb_sc):
    i = pl.program_id(2)
    q = q_ref[0, 0, 0]
    uo = uo_ref[...]
    acc_sc[...] = jnp.zeros_like(acc_sc)
    cb_sc[...] = jnp.zeros_like(cb_sc)

    def tile(j, diag):
        start = pl.multiple_of(j * TK, TK)
        kj = k_ref[0, 0, pl.ds(start, TK), :]
        vj = v_ref[0, 0, pl.ds(start, TK), :]
        z = _nt_dot(q, kj)
        before = None
        if diag:
            row = lax.broadcasted_iota(I32, (GT, TK), 0) & (TQ - 1)
            col = lax.broadcasted_iota(I32, (GT, TK), 1)
            before = col < row
        w, cb = _sb_tile(z, uo, cb_sc[...], before)
        acc_sc[...] += _dot(w.astype(BF16), vj)
        cb_sc[...] = cb
        return jnp.max(cb, axis=0, keepdims=True)[0, 0]

    live0 = tile(i, True)

    def cond(c):
        return jnp.logical_and(c[0] >= 0, c[1] > SB_DEAD_LOG)

    def body(c):
        return c[0] - 1, tile(c[0], False)

    lax.while_loop(cond, body, (i - 1, live0))
    o_ref[0, 0, 0] = acc_sc[...]


def sb_prompt_attention(q_st, k_hm, v_hm, uo):
    b, kvh, nq = q_st.shape[:3]
    s = k_hm.shape[2]
    kv_spec = pl.BlockSpec((1, 1, s, HEAD_DIM), lambda bb, kk, ii: (bb, kk, 0, 0))
    q_spec = pl.BlockSpec((1, 1, 1, GT, HEAD_DIM), lambda bb, kk, ii: (bb, kk, ii, 0, 0))
    return pl.pallas_call(
        _sb_prompt_kernel,
        grid=(b, kvh, nq),
        in_specs=[q_spec, kv_spec, kv_spec, pl.BlockSpec((TK, 2 * TK), lambda bb, kk, ii: (0, 0))],
        out_specs=q_spec,
        out_shape=jax.ShapeDtypeStruct(q_st.shape, F32),
        scratch_shapes=[pltpu.VMEM((GT, HEAD_DIM), F32), pltpu.VMEM((GT, TK), F32)],
        compiler_params=_cparams(("parallel", "parallel", "arbitrary")),
        name="sb_prompt",
    )(q_st, k_hm, v_hm, uo)


def _fold_heads(res):
    kv_of_row = lax.broadcasted_iota(I32, (N_HEADS, HEAD_DIM), 0) >> GROUP_SHIFT
    out = jnp.zeros((N_HEADS, HEAD_DIM), F32)
    for k in range(N_KV_HEADS):
        out = out + jnp.where(kv_of_row == k, res[:, k * HEAD_DIM:(k + 1) * HEAD_DIM], 0.0)
    return out


def _sb_sample_kernel(pt_ref, q_ref, uo_ref, *rest):
    k_refs = rest[:PAGES_PER_STEP]
    v_refs = rest[PAGES_PER_STEP:2 * PAGES_PER_STEP]
    o_ref, acc_sc, cb_sc, dead_sc = rest[2 * PAGES_PER_STEP:]
    p = pl.program_id(1)

    @pl.when(p == 0)
    def _():
        acc_sc[...] = jnp.zeros_like(acc_sc)
        cb_sc[...] = jnp.zeros_like(cb_sc)
        dead_sc[0] = 0

    q = q_ref[0]
    uo = uo_ref[...]
    for i in range(PAGES_PER_STEP):
        @pl.when(dead_sc[0] == 0)
        def _(i=i):
            z = _nt_dot(q, k_refs[i][0].astype(BF16))
            w, cb = _sb_tile(z, uo, cb_sc[...], None)
            acc_sc[...] += _dot(w.astype(BF16), v_refs[i][0].astype(BF16))
            cb_sc[...] = cb
            live = jnp.max(cb, axis=0, keepdims=True)[0, 0]
            dead_sc[0] = (live <= SB_DEAD_LOG).astype(I32)

    @pl.when(p == pl.num_programs(1) - 1)
    def _():
        o_ref[0] = _fold_heads(acc_sc[...])


def sb_sample_attention(qbd, pool_k, pool_v, page_base, page_table, uo):
    b = qbd.shape[0]
    n_pages = page_table.shape[1]
    page = pool_k.shape[1]
    steps = n_pages // PAGES_PER_STEP

    def page_spec(i):
        return pl.BlockSpec(
            (1, page, KV_DIM),
            lambda bb, pp, pt: (page_base + pt[bb, n_pages - 1 - (pp * PAGES_PER_STEP + i)], 0, 0))

    pages = [page_spec(i) for i in range(PAGES_PER_STEP)]
    grid_spec = pltpu.PrefetchScalarGridSpec(
        num_scalar_prefetch=1,
        grid=(b, steps),
        in_specs=[pl.BlockSpec((1, N_HEADS, KV_DIM), lambda bb, pp, pt: (bb, 0, 0)),
                  pl.BlockSpec((page, 2 * page), lambda bb, pp, pt: (0, 0))] + pages + pages,
        out_specs=pl.BlockSpec((1, N_HEADS, HEAD_DIM), lambda bb, pp, pt: (bb, 0, 0)),
        scratch_shapes=[pltpu.VMEM((N_HEADS, KV_DIM), F32), pltpu.VMEM((N_HEADS, page), F32),
                        pltpu.SMEM((1,), I32)],
    )
    return pl.pallas_call(
        _sb_sample_kernel,
        grid_spec=grid_spec,
        out_shape=jax.ShapeDtypeStruct((b, N_HEADS, HEAD_DIM), F32),
        compiler_params=_cparams(("parallel", "arbitrary")),
        name="sb_sample",
    )(page_table, qbd, uo, *([pool_k] * PAGES_PER_STEP), *([pool_v] * PAGES_PER_STEP))


def _group_rows(vals, shape, axis):
    g = lax.broadcasted_iota(I32, shape, axis) >> TQ_SHIFT
    out = jnp.full(shape, vals[GROUP - 1], F32)
    for i in range(GROUP - 2, -1, -1):
        out = jnp.where(g == i, vals[i], out)
    return out


def _online_step(s, valid, vj, m_sc, l_sc, acc_sc):
    m_prev = m_sc[...]
    m_new = jnp.maximum(m_prev, jnp.max(s, axis=1, keepdims=True))
    alpha = jnp.exp(m_prev - m_new)
    p = jnp.where(valid, jnp.exp(s - m_new), 0.0)
    l_sc[...] = alpha * l_sc[...] + jnp.sum(p, axis=1, keepdims=True)
    acc_sc[...] = acc_sc[...] * alpha[:, :HEAD_DIM] + _dot(p.astype(BF16), vj)
    m_sc[...] = m_new


def _banded_kernel(window, has_sink, slopes_ref, sinks_ref, q_ref, k_ref, v_ref, o_ref,
                   m_sc, l_sc, acc_sc):
    kk = pl.program_id(1)
    i = pl.program_id(2)
    q = q_ref[0, 0, 0]
    slope = _group_rows([slopes_ref[kk * GROUP + g] for g in range(GROUP)], (GT, TK), 0)
    if has_sink:
        m_sc[...] = _group_rows([sinks_ref[kk * GROUP + g] for g in range(GROUP)], (GT, TK), 0)
        l_sc[...] = jnp.ones_like(l_sc)
    else:
        m_sc[...] = jnp.full_like(m_sc, NEG_INF)
        l_sc[...] = jnp.zeros_like(l_sc)
    acc_sc[...] = jnp.zeros_like(acc_sc)
    row = lax.broadcasted_iota(I32, (GT, TK), 0) & (TQ - 1)
    col = lax.broadcasted_iota(I32, (GT, TK), 1)
    n_back = window // TK
    for c in range(n_back + 1):
        jt = i - n_back + c

        @pl.when(jt >= 0)
        def _(jt=jt):
            start = pl.multiple_of(jt * TK, TK)
            kj = k_ref[0, 0, pl.ds(start, TK), :]
            vj = v_ref[0, 0, pl.ds(start, TK), :]
            dist = (i * TQ + row) - (jt * TK + col)
            valid = jnp.logical_and(dist >= 0, dist <= window)
            s = jnp.where(valid, _nt_dot(q, kj) - slope * dist.astype(F32), NEG_INF)
            _online_step(s, valid, vj, m_sc, l_sc, acc_sc)

    o_ref[0, 0, 0] = acc_sc[...] / l_sc[:, :HEAD_DIM]


def banded_attention(q_st, k_hm, v_hm, slopes, sinks, window, has_sink):
    b, kvh, nq = q_st.shape[:3]
    s = k_hm.shape[2]
    kv_spec = pl.BlockSpec((1, 1, s, HEAD_DIM), lambda bb, kk, ii: (bb, kk, 0, 0))
    q_spec = pl.BlockSpec((1, 1, 1, GT, HEAD_DIM), lambda bb, kk, ii: (bb, kk, ii, 0, 0))
    smem = pl.BlockSpec(memory_space=pltpu.SMEM)
    return pl.pallas_call(
        functools.partial(_banded_kernel, window, has_sink),
        grid=(b, kvh, nq),
        in_specs=[smem, smem, q_spec, kv_spec, kv_spec],
        out_specs=q_spec,
        out_shape=jax.ShapeDtypeStruct(q_st.shape, F32),
        scratch_shapes=[pltpu.VMEM((GT, TK), F32), pltpu.VMEM((GT, TK), F32),
                        pltpu.VMEM((GT, HEAD_DIM), F32)],
        compiler_params=_cparams(("parallel", "parallel", "parallel")),
        name="banded_attention",
    )(slopes, sinks, q_st, k_hm, v_hm)


def _nsa_selected_prompt_kernel(slopes_ref, q_ref, sel_ref, e_ref, k_ref, v_ref, o_ref,
                                m_sc, l_sc, acc_sc):
    kk = pl.program_id(1)
    i = pl.program_id(2)
    q = q_ref[0, 0, 0]
    sel = sel_ref[0, 0, 0]
    sel4 = jnp.concatenate([sel] * GROUP, axis=0)
    slope = _group_rows([slopes_ref[kk * GROUP + g] for g in range(GROUP)], (GT, TK), 0)
    m_sc[...] = jnp.full_like(m_sc, NEG_INF)
    l_sc[...] = jnp.zeros_like(l_sc)
    acc_sc[...] = jnp.zeros_like(acc_sc)
    row = lax.broadcasted_iota(I32, (GT, TK), 0) & (TQ - 1)
    col = lax.broadcasted_iota(I32, (GT, TK), 1)

    def body(j, carry):
        start = pl.multiple_of(j * TK, TK)
        kj = k_ref[0, 0, pl.ds(start, TK), :]
        vj = v_ref[0, 0, pl.ds(start, TK), :]
        picked = _dot(sel4, e_ref[j])
        dist = (i * TQ + row) - (j * TK + col)
        valid = jnp.logical_and(dist >= 0, picked > 0.5)
        s = jnp.where(valid, _nt_dot(q, kj) - slope * dist.astype(F32), NEG_INF)
        _online_step(s, valid, vj, m_sc, l_sc, acc_sc)
        return carry

    lax.fori_loop(0, i + 1, body, 0)
    o_ref[0, 0, 0] = acc_sc[...] / l_sc[:, :HEAD_DIM]


def nsa_selected_prompt(q_st, sel, expand, k_hm, v_hm, slopes):
    b, kvh, nq = q_st.shape[:3]
    s = k_hm.shape[2]
    nb = sel.shape[-1]
    kv_spec = pl.BlockSpec((1, 1, s, HEAD_DIM), lambda bb, kk, ii: (bb, kk, 0, 0))
    q_spec = pl.BlockSpec((1, 1, 1, GT, HEAD_DIM), lambda bb, kk, ii: (bb, kk, ii, 0, 0))
    return pl.pallas_call(
        _nsa_selected_prompt_kernel,
        grid=(b, kvh, nq),
        in_specs=[pl.BlockSpec(memory_space=pltpu.SMEM), q_spec,
                  pl.BlockSpec((1, 1, 1, TQ, nb), lambda bb, kk, ii: (bb, kk, ii, 0, 0)),
                  pl.BlockSpec(expand.shape, lambda bb, kk, ii: (0, 0, 0)),
                  kv_spec, kv_spec],
        out_specs=q_spec,
        out_shape=jax.ShapeDtypeStruct(q_st.shape, F32),
        scratch_shapes=[pltpu.VMEM((GT, TK), F32), pltpu.VMEM((GT, TK), F32),
                        pltpu.VMEM((GT, HEAD_DIM), F32)],
        compiler_params=_cparams(("parallel", "parallel", "parallel")),
        name="nsa_selected_prompt",
    )(slopes, q_st, sel, expand, k_hm, v_hm)


def _compress_rows(x, wb):
    n = x.shape[0] // NSA_BLOCK
    return jnp.sum(x.reshape(n, NSA_BLOCK, KV_DIM) * wb[None], axis=1)


def _compress_prompt_kernel(k_ref, v_ref, wk_ref, wv_ref, ok_ref, ov_ref):
    ok_ref[0] = _compress_rows(k_ref[0], wk_ref[...])
    ov_ref[0] = _compress_rows(v_ref[0], wv_ref[...])


def compress_prompt(kc, vc, wkb, wvb):
    b, s, _ = kc.shape
    nb = s // NSA_BLOCK
    x_spec = pl.BlockSpec((1, s, KV_DIM), lambda bb: (bb, 0, 0))
    w_spec = pl.BlockSpec((NSA_BLOCK, KV_DIM), lambda bb: (0, 0))
    o_spec = pl.BlockSpec((1, nb, KV_DIM), lambda bb: (bb, 0, 0))
    shape = jax.ShapeDtypeStruct((b, nb, KV_DIM), F32)
    return pl.pallas_call(
        _compress_prompt_kernel,
        grid=(b,),
        in_specs=[x_spec, x_spec, w_spec, w_spec],
        out_specs=[o_spec, o_spec],
        out_shape=[shape, shape],
        compiler_params=_cparams(("parallel",)),
        name="compress_prompt",
    )(kc, vc, wkb, wvb)


def _compress_sample_kernel(pt_ref, wk_ref, wv_ref, *rest):
    k_refs = rest[:PAGES_PER_STEP]
    v_refs = rest[PAGES_PER_STEP:2 * PAGES_PER_STEP]
    ok_ref, ov_ref = rest[2 * PAGES_PER_STEP:]
    per_page = k_refs[0].shape[1] // NSA_BLOCK
    for i in range(PAGES_PER_STEP):
        ok_ref[0, i * per_page:(i + 1) * per_page, :] = _compress_rows(k_refs[i][0], wk_ref[...])
        ov_ref[0, i * per_page:(i + 1) * per_page, :] = _compress_rows(v_refs[i][0], wv_ref[...])


def compress_sample(pool_k, pool_v, page_base, page_table, wkb, wvb):
    b, n_pages = page_table.shape
    page = pool_k.shape[1]
    per_step = PAGES_PER_STEP * page // NSA_BLOCK
    nb = n_pages * page // NSA_BLOCK

    def page_spec(i):
        return pl.BlockSpec((1, page, KV_DIM),
                            lambda bb, pp, pt: (page_base + pt[bb, pp * PAGES_PER_STEP + i], 0, 0))

    pages = [page_spec(i) for i in range(PAGES_PER_STEP)]
    w_spec = pl.BlockSpec((NSA_BLOCK, KV_DIM), lambda bb, pp, pt: (0, 0))
    o_spec = pl.BlockSpec((1, per_step, KV_DIM), lambda bb, pp, pt: (bb, pp, 0))
    shape = jax.ShapeDtypeStruct((b, nb, KV_DIM), F32)
    grid_spec = pltpu.PrefetchScalarGridSpec(
        num_scalar_prefetch=1,
        grid=(b, n_pages // PAGES_PER_STEP),
        in_specs=[w_spec, w_spec] + pages + pages,
        out_specs=[o_spec, o_spec],
    )
    return pl.pallas_call(
        _compress_sample_kernel,
        grid_spec=grid_spec,
        out_shape=[shape, shape],
        compiler_params=_cparams(("parallel", "parallel")),
        name="compress_sample",
    )(page_table, wkb, wvb, *([pool_k] * PAGES_PER_STEP), *([pool_v] * PAGES_PER_STEP))


def _rank_rows(score_ref):
    n = score_ref.shape[0]
    score = score_ref[...]
    idx = lax.broadcasted_iota(I32, score.shape, 0)
    rank = jnp.zeros(score.shape, F32)
    for m in range(n):
        row_m = score_ref[m:m + 1, :]
        beats = jnp.where(row_m > score, 1.0, jnp.where(row_m == score, jnp.where(idx > m, 1.0, 0.0), 0.0))
        rank = rank + beats
    return rank


def _nsa_cmp_prompt_kernel(slopes_ref, qt_ref, kc_ref, vct_ref, o_ref, sel_ref, score_sc):
    kk = pl.program_id(1)
    i = pl.program_id(2)
    nb = kc_ref.shape[2]
    s = _dot(kc_ref[0, 0], qt_ref[0, 0, 0])
    blk = lax.broadcasted_iota(I32, (nb, GT), 0)
    q_pos = i * TQ + (lax.broadcasted_iota(I32, (nb, GT), 1) & (TQ - 1))
    dist = q_pos - (blk * NSA_BLOCK + NSA_BLOCK - 1)
    valid = dist >= 0
    slope = _group_rows([slopes_ref[kk * GROUP + g] for g in range(GROUP)], (nb, GT), 1)
    s = jnp.where(valid, s - slope * dist.astype(F32), NEG_INF)
    m = jnp.max(s, axis=0, keepdims=True)
    p = jnp.where(valid, jnp.exp(s - m), 0.0)
    p = p / jnp.maximum(jnp.sum(p, axis=0, keepdims=True), TINY)
    o_ref[0, 0, 0] = _dot(vct_ref[0, 0], p.astype(BF16))
    imp = p[:, 0:TQ]
    for g in range(1, GROUP):
        imp = imp + p[:, g * TQ:(g + 1) * TQ]
    blk_q = lax.broadcasted_iota(I32, (nb, TQ), 0)
    cur = (i * TQ + lax.broadcasted_iota(I32, (nb, TQ), 1)) >> NSA_BLOCK_SHIFT
    score_sc[...] = jnp.where(blk_q == cur, FORCE_SCORE, jnp.where(blk_q < cur, imp, -1.0))
    rank = _rank_rows(score_sc)
    sel_ref[0, 0, 0] = jnp.where(rank < float(min(NSA_TOPK, nb)), 1.0, 0.0)


def nsa_cmp_prompt(qt_st, kcmp_hm, vcmp_t, slopes):
    b, kvh, nq = qt_st.shape[:3]
    nb = kcmp_hm.shape[2]
    qt_spec = pl.BlockSpec((1, 1, 1, HEAD_DIM, GT), lambda bb, kk, ii: (bb, kk, ii, 0, 0))
    return pl.pallas_call(
        _nsa_cmp_prompt_kernel,
        grid=(b, kvh, nq),
        in_specs=[pl.BlockSpec(memory_space=pltpu.SMEM), qt_spec,
                  pl.BlockSpec((1, 1, nb, HEAD_DIM), lambda bb, kk, ii: (bb, kk, 0, 0)),
                  pl.BlockSpec((1, 1, HEAD_DIM, nb), lambda bb, kk, ii: (bb, kk, 0, 0))],
        out_specs=[qt_spec, pl.BlockSpec((1, 1, 1, nb, TQ), lambda bb, kk, ii: (bb, kk, ii, 0, 0))],
        out_shape=[jax.ShapeDtypeStruct(qt_st.shape, F32),
                   jax.ShapeDtypeStruct((b, kvh, nq, nb, TQ), F32)],
        scratch_shapes=[pltpu.VMEM((nb, TQ), F32)],
        compiler_params=_cparams(("parallel", "parallel", "parallel")),
        name="nsa_cmp_prompt",
    )(slopes, qt_st, kcmp_hm, vcmp_t)


def _nsa_cmp_sample_kernel(past_len, q_ref, slope_ref, gsum_ref, kc_ref, vc_ref, o_ref, idx_ref):
    nb = kc_ref.shape[1]
    s = _nt_dot(q_ref[0], kc_ref[0].astype(BF16))
    blk = lax.broadcasted_iota(I32, (N_HEADS, nb), 1)
    dist = past_len - (blk * NSA_BLOCK + NSA_BLOCK - 1)
    valid = dist >= 0
    s = jnp.where(valid, s - slope_ref[...] * dist.astype(F32), NEG_INF)
    m = jnp.max(s, axis=1, keepdims=True)
    p = jnp.where(valid, jnp.exp(s - m), 0.0)
    p = p / jnp.maximum(jnp.sum(p, axis=1, keepdims=True), TINY)
    o_ref[0] = _fold_heads(_dot(p.astype(BF16), vc_ref[0].astype(BF16)))
    imp = jnp.dot(gsum_ref[...], p, preferred_element_type=F32, precision=lax.Precision.HIGHEST)
    rows = imp.shape[0]
    blk8 = lax.broadcasted_iota(I32, (rows, nb), 1)
    rank = jnp.zeros((rows, nb), F32)
    for mcol in range(nb):
        col_m = imp[:, mcol:mcol + 1]
        rank = rank + jnp.where(col_m > imp, 1.0,
                                jnp.where(col_m == imp, jnp.where(blk8 > mcol, 1.0, 0.0), 0.0))
    blk_f = blk8.astype(F32)
    out = jnp.zeros((rows, nb), F32)
    for r in range(NSA_TOPK - 1):
        idx_r = jnp.sum(jnp.where(rank == float(r), blk_f, 0.0), axis=1, keepdims=True)
        out = jnp.where(blk8 == r, idx_r, out)
    idx_ref[0] = out.astype(I32)


def nsa_cmp_sample(qbd, slope_rows, gsum, kcmp, vcmp, past_len):
    b = qbd.shape[0]
    nb = kcmp.shape[1]
    rows = gsum.shape[0]
    return pl.pallas_call(
        functools.partial(_nsa_cmp_sample_kernel, past_len),
        grid=(b,),
        in_specs=[pl.BlockSpec((1, N_HEADS, KV_DIM), lambda bb: (bb, 0, 0)),
                  pl.BlockSpec((N_HEADS, nb), lambda bb: (0, 0)),
                  pl.BlockSpec((rows, N_HEADS), lambda bb: (0, 0)),
                  pl.BlockSpec((1, nb, KV_DIM), lambda bb: (bb, 0, 0)),
                  pl.BlockSpec((1, nb, KV_DIM), lambda bb: (bb, 0, 0))],
        out_specs=[pl.BlockSpec((1, N_HEADS, HEAD_DIM), lambda bb: (bb, 0, 0)),
                   pl.BlockSpec((1, rows, nb), lambda bb: (bb, 0, 0))],
        out_shape=[jax.ShapeDtypeStruct((b, N_HEADS, HEAD_DIM), F32),
                   jax.ShapeDtypeStruct((b, rows, nb), I32)],
        compiler_params=_cparams(("parallel",)),
        name="nsa_cmp_sample",
    )(qbd, slope_rows, gsum, kcmp, vcmp)


def _nsa_selected_sample_kernel(past_len, phys_ref, logi_ref, q_ref, slope_ref, nk_ref, nv_ref, *rest):
    n_past = NSA_TOPK - 1
    k_refs = rest[:n_past]
    v_refs = rest[n_past:2 * n_past]
    o_ref = rest[2 * n_past]
    base = (pl.program_id(0) * N_KV_HEADS + pl.program_id(1)) * n_past
    q = q_ref[0]
    slope = slope_ref[...][:, :NSA_BLOCK]
    col = lax.broadcasted_iota(I32, (N_HEADS, NSA_BLOCK), 1)
    scores, valids, values = [], [], []
    for j in range(n_past):
        dist = past_len - (logi_ref[base + j] * NSA_BLOCK + col)
        valid = dist >= 0
        z = _nt_dot(q, k_refs[j][0].astype(BF16))
        scores.append(jnp.where(valid, z - slope * dist.astype(F32), NEG_INF))
        valids.append(valid)
        values.append(v_refs[j][0].astype(BF16))
    n_new = nk_ref.shape[1]
    valid = lax.broadcasted_iota(I32, (N_HEADS, n_new), 1) == 0
    scores.append(jnp.where(valid, _nt_dot(q, nk_ref[0].astype(BF16)), NEG_INF))
    valids.append(valid)
    values.append(nv_ref[0].astype(BF16))
    m = jnp.max(scores[0], axis=1, keepdims=True)
    for sc in scores[1:]:
        m = jnp.maximum(m, jnp.max(sc, axis=1, keepdims=True))
    denom = jnp.zeros((N_HEADS, 1), F32)
    acc = jnp.zeros((N_HEADS, KV_DIM), F32)
    for sc, valid, val in zip(scores, valids, values):
        p = jnp.where(valid, jnp.exp(sc - m), 0.0)
        denom = denom + jnp.sum(p, axis=1, keepdims=True)
        acc = acc + _dot(p.astype(BF16), val)
    o_ref[0, 0] = _fold_heads(acc / denom)


def nsa_selected_sample(qbd, slope_rows, phys_idx, logi_idx, blocks_k, blocks_v, new_k, new_v, past_len):
    b = qbd.shape[0]
    n_past = NSA_TOPK - 1
    n_new = new_k.shape[1]

    def blk_spec(j):
        return pl.BlockSpec(
            (1, NSA_BLOCK, KV_DIM),
            lambda bb, kk, phys, logi: (phys[(bb * N_KV_HEADS + kk) * n_past + j], 0, 0))

    blks = [blk_spec(j) for j in range(n_past)]
    new_spec = pl.BlockSpec((1, n_new, KV_DIM), lambda bb, kk, phys, logi: (bb, 0, 0))
    grid_spec = pltpu.PrefetchScalarGridSpec(
        num_scalar_prefetch=2,
        grid=(b, N_KV_HEADS),
        in_specs=[pl.BlockSpec((1, N_HEADS, KV_DIM), lambda bb, kk, phys, logi: (bb, 0, 0)),
                  pl.BlockSpec((N_HEADS, LANES), lambda bb, kk, phys, logi: (0, 0)),
                  new_spec, new_spec] + blks + blks,
        out_specs=pl.BlockSpec((1, 1, N_HEADS, HEAD_DIM), lambda bb, kk, phys, logi: (bb, kk, 0, 0)),
    )
    return pl.pallas_call(
        functools.partial(_nsa_selected_sample_kernel, past_len),
        grid_spec=grid_spec,
        out_shape=jax.ShapeDtypeStruct((b, N_KV_HEADS, N_HEADS, HEAD_DIM), F32),
        compiler_params=_cparams(("parallel", "parallel")),
        name="nsa_selected_sample",
    )(phys_idx, logi_idx, qbd, slope_rows, new_k, new_v,
      *([blocks_k] * n_past), *([blocks_v] * n_past))


def _window_sample_kernel(window, has_sink, q_ref, slope_ref, sink_ref, wk_ref, wv_ref,
                          nk_ref, nv_ref, o_ref):
    n_buf = wk_ref.shape[1]
    n_new = nk_ref.shape[1]
    q = q_ref[0]
    dist1 = n_buf - lax.broadcasted_iota(I32, (N_HEADS, n_buf), 1)
    valid1 = dist1 <= window
    s1 = jnp.where(valid1,
                   _nt_dot(q, wk_ref[0].astype(BF16)) - slope_ref[...][:, :1] * dist1.astype(F32),
                   NEG_INF)
    valid2 = lax.broadcasted_iota(I32, (N_HEADS, n_new), 1) == 0
    s2 = jnp.where(valid2, _nt_dot(q, nk_ref[0].astype(BF16)), NEG_INF)
    m = jnp.maximum(jnp.max(s1, axis=1, keepdims=True), jnp.max(s2, axis=1, keepdims=True))
    if has_sink:
        sink = sink_ref[...][:, :1]
        m = jnp.maximum(m, sink)
    p1 = jnp.where(valid1, jnp.exp(s1 - m), 0.0)
    p2 = jnp.where(valid2, jnp.exp(s2 - m), 0.0)
    denom = jnp.sum(p1, axis=1, keepdims=True) + jnp.sum(p2, axis=1, keepdims=True)
    if has_sink:
        denom = denom + jnp.exp(sink - m)
    acc = _dot(p1.astype(BF16), wv_ref[0].astype(BF16)) + _dot(p2.astype(BF16), nv_ref[0].astype(BF16))
    o_ref[0] = _fold_heads(acc / denom)


def window_sample_attention(qbd, slope_rows, sink_rows, win_k, win_v, new_k, new_v, window, has_sink):
    b, n_buf, _ = win_k.shape
    n_new = new_k.shape[1]
    const = pl.BlockSpec((N_HEADS, LANES), lambda bb: (0, 0))
    return pl.pallas_call(
        functools.partial(_window_sample_kernel, window, has_sink),
        grid=(b,),
        in_specs=[pl.BlockSpec((1, N_HEADS, KV_DIM), lambda bb: (bb, 0, 0)), const, const,
                  pl.BlockSpec((1, n_buf, KV_DIM), lambda bb: (bb, 0, 0)),
                  pl.BlockSpec((1, n_buf, KV_DIM), lambda bb: (bb, 0, 0)),
                  pl.BlockSpec((1, n_new, KV_DIM), lambda bb: (bb, 0, 0)),
                  pl.BlockSpec((1, n_new, KV_DIM), lambda bb: (bb, 0, 0))],
        out_specs=pl.BlockSpec((1, N_HEADS, HEAD_DIM), lambda bb: (bb, 0, 0)),
        out_shape=jax.ShapeDtypeStruct((b, N_HEADS, HEAD_DIM), F32),
        compiler_params=_cparams(("parallel",)),
        name="window_sample",
    )(qbd, slope_rows, sink_rows, win_k, win_v, new_k, new_v)


def _stack_queries(q, b, s):
    q = (q * ATTN_SCALE).astype(BF16).reshape(b, s // TQ, TQ, N_KV_HEADS, GROUP, HEAD_DIM)
    return q.transpose(0, 3, 1, 4, 2, 5).reshape(b, N_KV_HEADS, s // TQ, GT, HEAD_DIM)


def _unstack_outputs(o, b, s):
    o = o.reshape(b, N_KV_HEADS, s // TQ, GROUP, TQ, HEAD_DIM).transpose(0, 2, 4, 1, 3, 5)
    return o.reshape(b * s, Q_DIM)


def _head_major(x, b, s):
    return x.astype(BF16).reshape(b, s, N_KV_HEADS, HEAD_DIM).transpose(0, 2, 1, 3)


def _block_diag_queries(q):
    b = q.shape[0]
    qh = (q * ATTN_SCALE).astype(BF16).reshape(b, N_KV_HEADS, GROUP, 1, HEAD_DIM)
    eye = jnp.eye(N_KV_HEADS, dtype=BF16).reshape(N_KV_HEADS, 1, N_KV_HEADS, 1)
    return (qh * eye).reshape(b, N_HEADS, KV_DIM)


def _suffix_matrix(n):
    r = jnp.arange(n)
    strict = (r[:, None] > r[None, :]).astype(BF16)
    return jnp.concatenate([strict, jnp.ones((n, n), BF16)], axis=1)


def _new_token_block(x, rows):
    return jnp.pad(x[:, None, :], ((0, 0), (0, rows - 1), (0, 0)))


def kernel(x_prompt, x_sample, cache_sb_k, cache_sb_v, cache_nsa_cmp_k, cache_nsa_cmp_v,
           cache_nsa_sel_k, cache_nsa_sel_v, cache_nsa_win_k, cache_nsa_win_v, cache_swa_k,
           cache_swa_v, page_table, norm_mix, norm_ffn, norm_final, w_sb_qkv, w_sb_o, w_nsa_in,
           w_nsa_o, nsa_cmp_wk, nsa_cmp_wv, w_swa_qkv, w_swa_o, swa_sinks, w_ffn_up, w_ffn_down):
    b, s, d = x_prompt.shape
    db = x_sample.shape[0]
    depth = norm_mix.shape[0]
    n_pool, page = cache_sb_k.shape[1:3]
    n_pages = page_table.shape[1]
    past_len = n_pages * page
    nsa_buf = cache_nsa_win_k.shape[2]
    swa_buf = cache_swa_k.shape[2]
    nq = s // TQ

    heads = jnp.arange(1, N_HEADS + 1, dtype=F32)
    slopes = jnp.exp2(-8.0 * heads / N_HEADS)
    slope_rows = jnp.broadcast_to(slopes[:, None], (N_HEADS, LANES))
    uo = _suffix_matrix(TK)
    no_sinks = jnp.zeros((N_HEADS,), F32)

    xp = x_prompt.reshape(b * s, d)
    xs = x_sample.reshape(db, d)
    tm_p, tm_s = 256, db

    def pool_view(pool):
        return pool.reshape(pool.shape[0] * n_pool, page, KV_DIM)

    outs = {name: [] for name in (
        "sb_k_p", "sb_v_p", "sb_k_s", "sb_v_s", "cmp_k_p", "cmp_v_p", "cmp_k_s", "cmp_v_s",
        "sel_k_p", "sel_v_p", "sel_k_s", "sel_v_s", "win_k_p", "win_v_p", "win_k_s", "win_v_s",
        "swa_k_p", "swa_v_p", "swa_k_s", "swa_v_s")}

    def kv_leaf(x, bb, ss):
        return x.reshape(bb, ss, N_KV_HEADS, HEAD_DIM)

    ia = ib = ic = 0
    for i in range(depth):
        kind = i % N_MIXERS
        if kind == 0:
            w_in = w_sb_qkv[ia].astype(BF16)
            w_o = w_sb_o[ia].astype(BF16)
            pp = norm_proj(xp, norm_mix[i], w_in, tm_p)
            ps = norm_proj(xs, norm_mix[i], w_in, tm_s)
            kp, vp = pp[:, Q_DIM:Q_DIM + KV_DIM], pp[:, Q_DIM + KV_DIM:]
            ksn, vsn = ps[:, Q_DIM:Q_DIM + KV_DIM], ps[:, Q_DIM + KV_DIM:]
            o_p = sb_prompt_attention(_stack_queries(pp[:, :Q_DIM], b, s),
                                      _head_major(kp, b, s), _head_major(vp, b, s), uo)
            o_p = _unstack_outputs(o_p, b, s)
            o_s = sb_sample_attention(_block_diag_queries(ps[:, :Q_DIM]), pool_view(cache_sb_k),
                                      pool_view(cache_sb_v), ia * n_pool, page_table, uo)
            xp = out_proj(xp, o_p, w_o, tm_p)
            xs = out_proj(xs, o_s.reshape(db, Q_DIM), w_o, tm_s)
            outs["sb_k_p"].append(kv_leaf(kp, b, s))
            outs["sb_v_p"].append(kv_leaf(vp, b, s))
            outs["sb_k_s"].append(kv_leaf(ksn, db, 1))
            outs["sb_v_s"].append(kv_leaf(vsn, db, 1))
            ia += 1
        elif kind == 1:
            n_cols = w_nsa_in.shape[2]
            n_pad = (-n_cols) % LANES
            w_in = jnp.pad(w_nsa_in[ib], ((0, 0), (0, n_pad))).astype(BF16)
            w_o = w_nsa_o[ib].astype(BF16)
            wkb = jnp.repeat(nsa_cmp_wk[ib], HEAD_DIM, axis=1)
            wvb = jnp.repeat(nsa_cmp_wv[ib], HEAD_DIM, axis=1)
            pp = norm_proj(xp, norm_mix[i], w_in, tm_p)
            ps = norm_proj(xs, norm_mix[i], w_in, tm_s)

            def split(p):
                cols = [p[:, Q_DIM + j * KV_DIM:Q_DIM + (j + 1) * KV_DIM] for j in range(6)]
                gl = p[:, Q_DIM + 6 * KV_DIM:Q_DIM + 6 * KV_DIM + NSA_GATES * N_HEADS]
                gl = gl.reshape(p.shape[0], N_HEADS, NSA_GATES)
                gl = [jnp.repeat(gl[:, :, j], HEAD_DIM, axis=1) for j in range(NSA_GATES)]
                return p[:, :Q_DIM], cols, gl

            q_p, (kc, vc, ks, vs, kw, vw), gl_p = split(pp)
            q_st = _stack_queries(q_p, b, s)
            kcmp, vcmp = compress_prompt(kc.reshape(b, s, KV_DIM), vc.reshape(b, s, KV_DIM), wkb, wvb)
            nb = kcmp.shape[1]
            kcmp_hm = kcmp.astype(BF16).reshape(b, nb, N_KV_HEADS, HEAD_DIM).transpose(0, 2, 1, 3)
            vcmp_t = vcmp.astype(BF16).reshape(b, nb, N_KV_HEADS, HEAD_DIM).transpose(0, 2, 3, 1)
            o_cmp_t, sel_t = nsa_cmp_prompt(q_st.swapaxes(3, 4), kcmp_hm, vcmp_t, slopes)
            o_cmp = _unstack_outputs(o_cmp_t.swapaxes(3, 4), b, s)
            sel = sel_t.swapaxes(3, 4).astype(BF16)
            per_tile = TK // NSA_BLOCK
            expand = (jnp.arange(nb)[None, :, None]
                      == (jnp.arange(s // TK)[:, None, None] * per_tile
                          + jnp.arange(TK)[None, None, :] // NSA_BLOCK)).astype(BF16)
            o_sel = _unstack_outputs(
                nsa_selected_prompt(q_st, sel, expand, _head_major(ks, b, s), _head_major(vs, b, s),
                                    slopes), b, s)
            o_win = _unstack_outputs(
                banded_attention(q_st, _head_major(kw, b, s), _head_major(vw, b, s), slopes,
                                 no_sinks, NSA_WINDOW, False), b, s)
            xp = nsa_out_proj(xp, gl_p, (o_cmp, o_sel, o_win), w_o, tm_p)
            for name, val in (("cmp_k_p", kc), ("cmp_v_p", vc), ("sel_k_p", ks), ("sel_v_p", vs)):
                outs[name].append(kv_leaf(val, b, s))
            outs["win_k_p"].append(kv_leaf(kw, b, s)[:, s - nsa_buf:])
            outs["win_v_p"].append(kv_leaf(vw, b, s)[:, s - nsa_buf:])

            q_s, (kc, vc, ks, vs, kw, vw), gl_s = split(ps)
            qbd = _block_diag_queries(q_s)
            kcmp, vcmp = compress_sample(pool_view(cache_nsa_cmp_k), pool_view(cache_nsa_cmp_v),
                                         ib * n_pool, page_table, wkb, wvb)
            gsum = (jnp.arange(8)[:, None] == jnp.arange(N_HEADS)[None, :] // GROUP).astype(F32)
            o_cmp, idx = nsa_cmp_sample(qbd, slope_rows, gsum, kcmp, vcmp, past_len)
            idx = idx[:, :N_KV_HEADS, :NSA_TOPK - 1]
            per_page = page // NSA_BLOCK
            n_blocks = cache_nsa_sel_k.shape[0] * n_pool * per_page
            phys = ((ib * n_pool + jnp.take_along_axis(page_table[:, None, :], idx // per_page, axis=2))
                    * per_page + idx % per_page).reshape(-1).astype(I32)
            o_sel = nsa_selected_sample(qbd, slope_rows, phys, idx.reshape(-1),
                                        cache_nsa_sel_k.reshape(n_blocks, NSA_BLOCK, KV_DIM),
                                        cache_nsa_sel_v.reshape(n_blocks, NSA_BLOCK, KV_DIM),
                                        _new_token_block(ks, 8), _new_token_block(vs, 8), past_len)
            o_sel = jnp.stack([o_sel[:, k, k * GROUP:(k + 1) * GROUP] for k in range(N_KV_HEADS)], axis=1)
            win_k = cache_nsa_win_k[ib].reshape(db, nsa_buf, KV_DIM)
            win_v = cache_nsa_win_v[ib].reshape(db, nsa_buf, KV_DIM)
            o_win = window_sample_attention(qbd, slope_rows, slope_rows, win_k, win_v,
                                            _new_token_block(kw, 8), _new_token_block(vw, 8),
                                            NSA_WINDOW, False)
            xs = nsa_out_proj(xs, gl_s, (o_cmp.reshape(db, Q_DIM), o_sel.reshape(db, Q_DIM),
                                         o_win.reshape(db, Q_DIM)), w_o, tm_s)
            for name, val in (("cmp_k_s", kc), ("cmp_v_s", vc), ("sel_k_s", ks), ("sel_v_s", vs)):
                outs[name].append(kv_leaf(val, db, 1))
            outs["win_k_s"].append(kv_leaf(jnp.concatenate([win_k[:, 1:], kw[:, None]], axis=1), db, nsa_buf))
            outs["win_v_s"].append(kv_leaf(jnp.concatenate([win_v[:, 1:], vw[:, None]], axis=1), db, nsa_buf))
            ib += 1
        else:
            w_in = w_swa_qkv[ic].astype(BF16)
            w_o = w_swa_o[ic].astype(BF16)
            sinks = swa_sinks[ic]
            pp = norm_proj(xp, norm_mix[i], w_in, tm_p)
            ps = norm_proj(xs, norm_mix[i], w_in, tm_s)
            kp, vp = pp[:, Q_DIM:Q_DIM + KV_DIM], pp[:, Q_DIM + KV_DIM:]
            ksn, vsn = ps[:, Q_DIM:Q_DIM + KV_DIM], ps[:, Q_DIM + KV_DIM:]
            o_p = banded_attention(_stack_queries(pp[:, :Q_DIM], b, s), _head_major(kp, b, s),
                                   _head_major(vp, b, s), slopes, sinks, SWA_WINDOW, True)
            o_p = _unstack_outputs(o_p, b, s)
            buf_k = cache_swa_k[ic].reshape(db, swa_buf, KV_DIM)
            buf_v = cache_swa_v[ic].reshape(db, swa_buf, KV_DIM)
            sink_rows = jnp.broadcast_to(sinks[:, None], (N_HEADS, LANES))
            o_s = window_sample_attention(_block_diag_queries(ps[:, :Q_DIM]), slope_rows, sink_rows,
                                          buf_k, buf_v, _new_token_block(ksn, 8),
                                          _new_token_block(vsn, 8), SWA_WINDOW, True)
            xp = out_proj(xp, o_p, w_o, tm_p)
            xs = out_proj(xs, o_s.reshape(db, Q_DIM), w_o, tm_s)
            outs["swa_k_p"].append(kv_leaf(kp, b, s)[:, s - swa_buf:])
            outs["swa_v_p"].append(kv_leaf(vp, b, s)[:, s - swa_buf:])
            outs["swa_k_s"].append(kv_leaf(jnp.concatenate([buf_k[:, 1:], ksn[:, None]], axis=1), db, swa_buf))
            outs["swa_v_s"].append(kv_leaf(jnp.concatenate([buf_v[:, 1:], vsn[:, None]], axis=1), db, swa_buf))
            ic += 1
        w_up = w_ffn_up[i].astype(BF16)
        w_down = w_ffn_down[i].astype(BF16)
        xp = ffn(xp, norm_ffn[i], w_up, w_down, 512, 512)
        xs = ffn(xs, norm_ffn[i], w_up, w_down, tm_s, 512)

    y_prompt = final_norm(xp, norm_final, tm_p).reshape(b, s, d)
    y_sample = final_norm(xs, norm_final, tm_s).reshape(db, 1, d)
    st = {name: jnp.stack(vals) for name, vals in outs.items()}
    return (y_prompt, y_sample,
            st["sb_k_p"], st["sb_v_p"], st["sb_k_s"], st["sb_v_s"],
            st["cmp_k_p"], st["cmp_v_p"], st["cmp_k_s"], st["cmp_v_s"],
            st["sel_k_p"], st["sel_v_p"], st["sel_k_s"], st["sel_v_s"],
            st["win_k_p"], st["win_v_p"], st["win_k_s"], st["win_v_s"],
            st["swa_k_p"], st["swa_v_p"], st["swa_k_s"], st["swa_v_s"])
```

```python
import functools
import math

import jax
import jax.numpy as jnp
from jax import lax
from jax.experimental import pallas as pl
from jax.experimental.pallas import tpu as pltpu

F32 = jnp.float32
BF16 = jnp.bfloat16
I32 = jnp.int32

D_MODEL = 1024
HEAD_DIM = 64
N_HEADS = 16
N_KV_HEADS = 4
GROUP = 4
Q_DIM = N_HEADS * HEAD_DIM
KV_DIM = N_KV_HEADS * HEAD_DIM
N_MIXERS = 3
NSA_BLOCK = 64
NSA_TOPK = 16
NSA_WINDOW = 512
SWA_WINDOW = 128
NSA_GATES = 3
ATTN_SCALE = HEAD_DIM ** -0.5
RMS_EPS = 1e-6
NEG_INF = -1e30
TINY = 1e-30
FORCE_SCORE = float(GROUP + 1)

LANES = 128
TQ = 128
TK = 128
GT = GROUP * TQ
GROUP_SHIFT = GROUP.bit_length() - 1
TQ_SHIFT = TQ.bit_length() - 1
NSA_BLOCK_SHIFT = NSA_BLOCK.bit_length() - 1
N_SLOPE_TERMS = 3
SB_HEAD_PAGES = 4
SB_TAIL_PAGES_PER_STEP = 15
COMPRESS_PAGES_PER_STEP = 8
NEW_ROWS = 8
SB_DEAD_LOG = -104.0
VMEM_LIMIT = 48 * 1024 * 1024


def _cparams(sem):
    return pltpu.CompilerParams(dimension_semantics=sem, vmem_limit_bytes=VMEM_LIMIT)


def _nt_dot(a, b):
    return lax.dot_general(a, b, (((1,), (1,)), ((), ())), preferred_element_type=F32)


def _dot(a, b):
    return jnp.dot(a, b, preferred_element_type=F32)


def _split_bf16(x):
    hi = x.astype(BF16)
    return hi, (x - hi.astype(F32)).astype(BF16)


def _rms(x, g):
    ms = jnp.mean(x * x, axis=-1, keepdims=True)
    return (x * lax.rsqrt(ms + RMS_EPS)) * g


def _largest_divisor(n, cap):
    return max(d for d in range(1, cap + 1) if n % d == 0)


def _proj_kernel(n_t, has_gate, x_ref, g_ref, wq_ref, wt_ref, *rest):
    if has_gate:
        wg_ref, rest = rest[0], rest[1:]
    q_ref, t_refs = rest[0], rest[1:1 + n_t]
    h = _rms(x_ref[...], g_ref[...]).astype(BF16)
    q_ref[...] = (_dot(h, wq_ref[...]) * ATTN_SCALE).astype(BF16)
    for j in range(n_t):
        t_refs[j][0] = _nt_dot(wt_ref[j * KV_DIM:(j + 1) * KV_DIM, :], h)
    if has_gate:
        rest[1 + n_t][...] = _dot(h, wg_ref[...])


def norm_proj(x, g, wq, wt, wg, b, tm):
    m, d = x.shape
    s = m // b
    per_b = s // tm
    n_t = wt.shape[0] // KV_DIM
    has_gate = wg is not None
    full = lambda a: pl.BlockSpec(a.shape, lambda i: (0,) * a.ndim)
    in_specs = [pl.BlockSpec((tm, d), lambda i: (i, 0)), pl.BlockSpec((1, d), lambda i: (0, 0)),
                full(wq), full(wt)]
    args = [x, g.reshape(1, d), wq, wt]
    out_specs = [pl.BlockSpec((tm, Q_DIM), lambda i: (i, 0))]
    out_shape = [jax.ShapeDtypeStruct((m, Q_DIM), BF16)]
    for _ in range(n_t):
        out_specs.append(pl.BlockSpec((1, KV_DIM, tm), lambda i: (i // per_b, 0, i % per_b)))
        out_shape.append(jax.ShapeDtypeStruct((b, KV_DIM, s), F32))
    if has_gate:
        in_specs.append(full(wg))
        args.append(wg)
        out_specs.append(pl.BlockSpec((tm, LANES), lambda i: (i, 0)))
        out_shape.append(jax.ShapeDtypeStruct((m, LANES), F32))
    return pl.pallas_call(
        functools.partial(_proj_kernel, n_t, has_gate),
        grid=(m // tm,),
        in_specs=in_specs,
        out_specs=out_specs,
        out_shape=out_shape,
        compiler_params=_cparams(("parallel",)),
        name="norm_proj",
    )(*args)


def _out_proj_kernel(res_ref, a_ref, w_ref, o_ref):
    o_ref[...] = res_ref[...] + _dot(a_ref[...].astype(BF16), w_ref[...])


def out_proj(res, a, w, tm):
    m, d = res.shape
    k = a.shape[1]
    return pl.pallas_call(
        _out_proj_kernel,
        grid=(m // tm,),
        in_specs=[pl.BlockSpec((tm, d), lambda i: (i, 0)),
                  pl.BlockSpec((tm, k), lambda i: (i, 0)),
                  pl.BlockSpec((k, d), lambda i: (0, 0))],
        out_specs=pl.BlockSpec((tm, d), lambda i: (i, 0)),
        out_shape=jax.ShapeDtypeStruct((m, d), F32),
        compiler_params=_cparams(("parallel",)),
        name="out_proj",
    )(res, a, w)


def _nsa_out_proj_kernel(res_ref, gl_ref, x_ref, a0_ref, a1_ref, a2_ref, w_ref, o_ref):
    hi, lo = _split_bf16(jax.nn.sigmoid(gl_ref[...]))
    o = jnp.zeros(res_ref.shape, F32)
    for j, a_ref in enumerate((a0_ref, a1_ref, a2_ref)):
        gate = _dot(hi, x_ref[j]) + _dot(lo, x_ref[j])
        o = o + gate * a_ref[...]
    o_ref[...] = res_ref[...] + _dot(o.astype(BF16), w_ref[...])


def nsa_out_proj(res, gate_logits, gate_expand, branches, w, tm):
    m, d = res.shape
    row = pl.BlockSpec((tm, d), lambda i: (i, 0))
    return pl.pallas_call(
        _nsa_out_proj_kernel,
        grid=(m // tm,),
        in_specs=[row, pl.BlockSpec((tm, LANES), lambda i: (i, 0)),
                  pl.BlockSpec(gate_expand.shape, lambda i: (0, 0, 0)), row, row, row,
                  pl.BlockSpec((d, d), lambda i: (0, 0))],
        out_specs=row,
        out_shape=jax.ShapeDtypeStruct((m, d), F32),
        compiler_params=_cparams(("parallel",)),
        name="nsa_out_proj",
    )(res, gate_logits, gate_expand, *branches, w)


def _ffn_kernel(x_ref, g_ref, wu_ref, wd_ref, o_ref, h_sc, acc_sc):
    f = pl.program_id(1)

    @pl.when(f == 0)
    def _():
        h_sc[...] = _rms(x_ref[...], g_ref[...]).astype(BF16)
        acc_sc[...] = jnp.zeros_like(acc_sc)

    u = jnp.maximum(_dot(h_sc[...], wu_ref[...]), 0.0)
    acc_sc[...] += _dot((u * u).astype(BF16), wd_ref[...])

    @pl.when(f == pl.num_programs(1) - 1)
    def _():
        o_ref[...] = x_ref[...] + acc_sc[...]


def ffn(x, g, w_up, w_down, tm, tf):
    m, d = x.shape
    dff = w_up.shape[1]
    return pl.pallas_call(
        _ffn_kernel,
        grid=(m // tm, dff // tf),
        in_specs=[pl.BlockSpec((tm, d), lambda i, f: (i, 0)),
                  pl.BlockSpec((1, d), lambda i, f: (0, 0)),
                  pl.BlockSpec((d, tf), lambda i, f: (0, f)),
                  pl.BlockSpec((tf, d), lambda i, f: (f, 0))],
        out_specs=pl.BlockSpec((tm, d), lambda i, f: (i, 0)),
        out_shape=jax.ShapeDtypeStruct((m, d), F32),
        scratch_shapes=[pltpu.VMEM((tm, d), BF16), pltpu.VMEM((tm, d), F32)],
        compiler_params=_cparams(("parallel", "arbitrary")),
        name="ffn",
    )(x, g.reshape(1, d), w_up, w_down)


def _final_norm_kernel(x_ref, g_ref, o_ref):
    o_ref[...] = _rms(x_ref[...], g_ref[...])


def final_norm(x, g, tm):
    m, d = x.shape
    return pl.pallas_call(
        _final_norm_kernel,
        grid=(m // tm,),
        in_specs=[pl.BlockSpec((tm, d), lambda i: (i, 0)), pl.BlockSpec((1, d), lambda i: (0, 0))],
        out_specs=pl.BlockSpec((tm, d), lambda i: (i, 0)),
        out_shape=jax.ShapeDtypeStruct((m, d), F32),
        compiler_params=_cparams(("parallel",)),
        name="final_norm",
    )(x, g.reshape(1, d))


def _stack_heads(q_ref):
    q = q_ref[0].astype(F32)
    return jnp.concatenate([q[:, g * HEAD_DIM:(g + 1) * HEAD_DIM] for g in range(GROUP)],
                           axis=0).astype(BF16)


def _unstack_heads(acc):
    return jnp.concatenate([acc[g * TQ:(g + 1) * TQ] for g in range(GROUP)], axis=1)


def _key_tile(ref, j):
    return ref[0, 0, :, pl.ds(pl.multiple_of(j * TK, TK), TK)].astype(BF16)


def _group_rows(vals, shape, axis):
    g = lax.broadcasted_iota(I32, shape, axis) >> TQ_SHIFT
    out = jnp.full(shape, vals[GROUP - 1], F32)
    for i in range(GROUP - 2, -1, -1):
        out = jnp.where(g == i, vals[i], out)
    return out


def _kv_slopes(slopes_ref, kk, shape, axis):
    return _group_rows([slopes_ref[kk * GROUP + g] for g in range(GROUP)], shape, axis)


def _slope_columns(slope):
    lane = lax.broadcasted_iota(I32, slope.shape, 1)
    out = jnp.zeros(slope.shape, F32)
    rem = slope
    for t in range(N_SLOPE_TERMS):
        piece = rem.astype(BF16).astype(F32)
        out = jnp.where(lane == t, piece, out)
        rem = rem - piece
    return out.astype(BF16)


def _position_rows():
    row = lax.broadcasted_iota(I32, (HEAD_DIM, TK), 0)
    col = lax.broadcasted_iota(I32, (HEAD_DIM, TK), 1)
    return jnp.where(row < N_SLOPE_TERMS, col, 0).astype(F32).astype(BF16)


def _online_step(s, tile_shift, vj, m_sc, l_sc, acc_sc):
    m_prev = m_sc[...]
    m_new = jnp.maximum(m_prev, jnp.max(s, axis=1, keepdims=True) - tile_shift)
    alpha = jnp.exp(m_prev - m_new)
    p = jnp.exp(s - (m_new + tile_shift))
    l_sc[...] = alpha * l_sc[...] + jnp.sum(p, axis=1, keepdims=True)
    acc_sc[...] = acc_sc[...] * alpha[:, :HEAD_DIM] + _nt_dot(p.astype(BF16), vj)
    m_sc[...] = m_new


def _sb_tile(z, uo, cb, before):
    sp = jnp.maximum(z, 0.0) + jnp.log(1.0 + jnp.exp(-jnp.abs(z)))
    lm = -sp
    if before is not None:
        lm = jnp.where(before, lm, 0.0)
    hi, lo = _split_bf16(lm)
    r = _dot(hi, uo) + _dot(lo, uo)
    w = jnp.exp((z - sp) + r[:, :TK] + cb)
    if before is not None:
        w = jnp.where(before, w, 0.0)
    return w, cb + r[:, TK:]


def _sb_prompt_kernel(q_ref, k_ref, v_ref, uo_ref, o_ref, acc_sc, cb_sc):
    i = pl.program_id(2)
    q = _stack_heads(q_ref)
    uo = uo_ref[...]
    acc_sc[...] = jnp.zeros_like(acc_sc)
    cb_sc[...] = jnp.zeros_like(cb_sc)

    def tile(j, diag):
        z = _dot(q, _key_tile(k_ref, j))
        before = None
        if diag:
            row = lax.broadcasted_iota(I32, (GT, TK), 0) & (TQ - 1)
            col = lax.broadcasted_iota(I32, (GT, TK), 1)
            before = col < row
        w, cb = _sb_tile(z, uo, cb_sc[...], before)
        acc_sc[...] += _nt_dot(w.astype(BF16), _key_tile(v_ref, j))
        cb_sc[...] = cb
        return jnp.max(cb, axis=0, keepdims=True)[0, 0]

    live0 = tile(i, True)

    def cond(c):
        return jnp.logical_and(c[0] >= 0, c[1] > SB_DEAD_LOG)

    def body(c):
        return c[0] - 1, tile(c[0], False)

    lax.while_loop(cond, body, (i - 1, live0))
    o_ref[0] = _unstack_heads(acc_sc[...]).astype(o_ref.dtype)


def _prompt_specs(b, s):
    q_spec = pl.BlockSpec((1, TQ, KV_DIM), lambda bb, kk, ii, *_: (bb, ii, kk))
    kv_spec = pl.BlockSpec((1, 1, HEAD_DIM, s), lambda bb, kk, ii, *_: (bb, kk, 0, 0))
    return q_spec, kv_spec


def sb_prompt_attention(q, kt, vt, uo):
    b, s, _ = q.shape
    q_spec, kv_spec = _prompt_specs(b, s)
    return pl.pallas_call(
        _sb_prompt_kernel,
        grid=(b, N_KV_HEADS, s // TQ),
        in_specs=[q_spec, kv_spec, kv_spec, pl.BlockSpec((TK, 2 * TK), lambda bb, kk, ii: (0, 0))],
        out_specs=q_spec,
        out_shape=jax.ShapeDtypeStruct((b, s, Q_DIM), BF16),
        scratch_shapes=[pltpu.VMEM((GT, HEAD_DIM), F32), pltpu.VMEM((GT, TK), F32)],
        compiler_params=_cparams(("parallel", "parallel", "arbitrary")),
        name="sb_prompt",
    )(q, kt, vt, uo)


def _fold_heads(res):
    kv_of_row = lax.broadcasted_iota(I32, (N_HEADS, HEAD_DIM), 0) >> GROUP_SHIFT
    out = jnp.zeros((N_HEADS, HEAD_DIM), F32)
    for k in range(N_KV_HEADS):
        out = out + jnp.where(kv_of_row == k, res[:, k * HEAD_DIM:(k + 1) * HEAD_DIM], 0.0)
    return out


def _sb_sample_kernel(n_pages_step, pt_ref, dead_ref, q_ref, uo_ref, acc_in_ref, cb_in_ref, *rest):
    k_refs = rest[:n_pages_step]
    v_refs = rest[n_pages_step:2 * n_pages_step]
    o_ref, acc_ref, cb_ref, dead_sc = rest[2 * n_pages_step:]
    p = pl.program_id(1)

    @pl.when(p == 0)
    def _():
        acc_ref[0] = acc_in_ref[0]
        cb_ref[0] = cb_in_ref[0]
        dead_sc[0] = dead_ref[pl.program_id(0)]

    q = q_ref[0]
    uo = uo_ref[...]
    for i in range(n_pages_step):
        @pl.when(dead_sc[0] == 0)
        def _(i=i):
            z = _dot(q, k_refs[i][0].astype(BF16))
            w, cb = _sb_tile(z, uo, cb_ref[0], None)
            acc_ref[0] += _nt_dot(w.astype(BF16), v_refs[i][0].astype(BF16))
            cb_ref[0] = cb
            live = jnp.max(cb, axis=0, keepdims=True)[0, 0]
            dead_sc[0] = (live <= SB_DEAD_LOG).astype(I32)

    @pl.when(p == pl.num_programs(1) - 1)
    def _():
        o_ref[0] = _fold_heads(acc_ref[0])


def sb_sample_pass(qbd, pool_k, pool_v, page_base, page_table, dead, acc, cb, uo, first_page, n_steps,
                   n_pages_step):
    b = qbd.shape[0]
    page = pool_k.shape[2]

    def page_spec(i):
        def index(bb, pp, pt, dd):
            pg = pt[bb, first_page - (pp * n_pages_step + i)]
            return (page_base + pg * (1 - dd[bb]), 0, 0)
        return pl.BlockSpec((1, KV_DIM, page), index)

    pages = [page_spec(i) for i in range(n_pages_step)]
    per_b = lambda shape: pl.BlockSpec((1,) + shape, lambda bb, pp, pt, dd: (bb, 0, 0))
    grid_spec = pltpu.PrefetchScalarGridSpec(
        num_scalar_prefetch=2,
        grid=(b, n_steps),
        in_specs=[per_b((N_HEADS, KV_DIM)),
                  pl.BlockSpec((page, 2 * page), lambda bb, pp, pt, dd: (0, 0)),
                  per_b((N_HEADS, KV_DIM)), per_b((N_HEADS, page))] + pages + pages,
        out_specs=[per_b((N_HEADS, HEAD_DIM)), per_b((N_HEADS, KV_DIM)), per_b((N_HEADS, page))],
        scratch_shapes=[pltpu.SMEM((1,), I32)],
    )
    return pl.pallas_call(
        functools.partial(_sb_sample_kernel, n_pages_step),
        grid_spec=grid_spec,
        out_shape=[jax.ShapeDtypeStruct((b, N_HEADS, HEAD_DIM), F32),
                   jax.ShapeDtypeStruct((b, N_HEADS, KV_DIM), F32),
                   jax.ShapeDtypeStruct((b, N_HEADS, page), F32)],
        compiler_params=_cparams(("parallel", "arbitrary")),
        name="sb_sample",
    )(page_table, dead, qbd, uo, acc, cb, *([pool_k] * n_pages_step), *([pool_v] * n_pages_step))


def sb_sample_attention(qbd, pool_k, pool_v, page_base, page_table, uo):
    b = qbd.shape[0]
    n_pages = page_table.shape[1]
    page = pool_k.shape[2]
    head = min(SB_HEAD_PAGES, n_pages)
    acc = jnp.zeros((b, N_HEADS, KV_DIM), F32)
    cb = jnp.zeros((b, N_HEADS, page), F32)
    dead = jnp.zeros((b,), I32)
    o, acc, cb = sb_sample_pass(qbd, pool_k, pool_v, page_base, page_table, dead, acc, cb, uo,
                                n_pages - 1, 1, head)
    tail = n_pages - head
    if tail:
        per_step = _largest_divisor(tail, SB_TAIL_PAGES_PER_STEP)
        dead = (jnp.max(cb, axis=(1, 2)) <= SB_DEAD_LOG).astype(I32)
        o, _, _ = sb_sample_pass(qbd, pool_k, pool_v, page_base, page_table, dead, acc, cb, uo,
                                 tail - 1, tail // per_step, per_step)
    return o


def _banded_kernel(window, has_sink, slopes_ref, sinks_ref, q_ref, k_ref, v_ref, o_ref,
                   m_sc, l_sc, acc_sc):
    kk = pl.program_id(1)
    i = pl.program_id(2)
    slope = _kv_slopes(slopes_ref, kk, (GT, TK), 0)
    qa = jnp.concatenate([_stack_heads(q_ref), _slope_columns(slope[:, :HEAD_DIM])], axis=1)
    pos_rows = _position_rows()
    row = lax.broadcasted_iota(I32, (GT, TK), 0) & (TQ - 1)
    col = lax.broadcasted_iota(I32, (GT, TK), 1)
    if has_sink:
        sink = _group_rows([sinks_ref[kk * GROUP + g] for g in range(GROUP)], (GT, TK), 0)
        m_sc[...] = sink + slope * row.astype(F32)
        l_sc[...] = jnp.ones_like(l_sc)
    else:
        m_sc[...] = jnp.full_like(m_sc, NEG_INF)
        l_sc[...] = jnp.zeros_like(l_sc)
    acc_sc[...] = jnp.zeros_like(acc_sc)
    n_back = window // TK
    for c in range(n_back + 1):
        jt = i - n_back + c

        @pl.when(jt >= 0)
        def _(jt=jt, c=c):
            ka = jnp.concatenate([_key_tile(k_ref, jt), pos_rows], axis=0)
            s = _dot(qa, ka)
            if c == 0:
                s = jnp.where(col >= row, s, NEG_INF)
            if c == n_back:
                s = jnp.where(col <= row, s, NEG_INF)
            _online_step(s, slope * float(TK * (n_back - c)), _key_tile(v_ref, jt), m_sc, l_sc, acc_sc)

    o_ref[0] = _unstack_heads(acc_sc[...] / l_sc[:, :HEAD_DIM]).astype(o_ref.dtype)


def banded_attention(q, kt, vt, slopes, sinks, window, has_sink, out_dtype):
    b, s, _ = q.shape
    assert window % TK == 0
    q_spec, kv_spec = _prompt_specs(b, s)
    smem = pl.BlockSpec(memory_space=pltpu.SMEM)
    return pl.pallas_call(
        functools.partial(_banded_kernel, window, has_sink),
        grid=(b, N_KV_HEADS, s // TQ),
        in_specs=[smem, smem, q_spec, kv_spec, kv_spec],
        out_specs=q_spec,
        out_shape=jax.ShapeDtypeStruct((b, s, Q_DIM), out_dtype),
        scratch_shapes=[pltpu.VMEM((GT, TK), F32), pltpu.VMEM((GT, TK), F32),
                        pltpu.VMEM((GT, HEAD_DIM), F32)],
        compiler_params=_cparams(("parallel", "parallel", "parallel")),
        name="banded_attention",
    )(slopes, sinks, q, kt, vt)


def _nsa_selected_prompt_kernel(flags_ref, slopes_ref, q_ref, sel_ref, e_ref, k_ref, v_ref, o_ref,
                                m_sc, l_sc, acc_sc):
    bb = pl.program_id(0)
    kk = pl.program_id(1)
    i = pl.program_id(2)
    nq = pl.num_programs(2)
    slope = _kv_slopes(slopes_ref, kk, (GT, TK), 0)
    sel = sel_ref[0, 0].astype(F32)
    off = jnp.concatenate([(sel - 1.0) * (-NEG_INF)] * GROUP, axis=0).astype(BF16)
    qa = jnp.concatenate([_stack_heads(q_ref), _slope_columns(slope[:, :HEAD_DIM]), off], axis=1)
    pos_rows = _position_rows()
    tile_step = slope * float(TK)
    m_sc[...] = jnp.full_like(m_sc, NEG_INF)
    l_sc[...] = jnp.zeros_like(l_sc)
    acc_sc[...] = jnp.zeros_like(acc_sc)
    tile_bits = flags_ref[(bb * N_KV_HEADS + kk) * nq + i]

    def scores(j):
        ka = jnp.concatenate([_key_tile(k_ref, j), pos_rows, e_ref[j]], axis=0)
        return _dot(qa, ka)

    def body(j, carry):
        @pl.when(((tile_bits >> j) & 1) != 0)
        def _():
            _online_step(scores(j), tile_step * (i - j).astype(F32), _key_tile(v_ref, j),
                         m_sc, l_sc, acc_sc)
        return carry

    lax.fori_loop(0, i, body, 0)
    row = lax.broadcasted_iota(I32, (GT, TK), 0) & (TQ - 1)
    col = lax.broadcasted_iota(I32, (GT, TK), 1)
    s = jnp.where(col <= row, scores(i), NEG_INF)
    _online_step(s, jnp.zeros((GT, TK), F32), _key_tile(v_ref, i), m_sc, l_sc, acc_sc)
    o_ref[0] = _unstack_heads(acc_sc[...] / l_sc[:, :HEAD_DIM])


def nsa_selected_prompt(q, sel, flags, expand, kt, vt, slopes):
    b, s, _ = q.shape
    assert s // TK <= 32
    nb = sel.shape[-1]
    q_spec, kv_spec = _prompt_specs(b, s)
    grid_spec = pltpu.PrefetchScalarGridSpec(
        num_scalar_prefetch=1,
        grid=(b, N_KV_HEADS, s // TQ),
        in_specs=[pl.BlockSpec(memory_space=pltpu.SMEM), q_spec,
                  pl.BlockSpec((1, 1, TQ, nb), lambda bb, kk, ii, fl: (bb, kk, ii, 0)),
                  pl.BlockSpec(expand.shape, lambda bb, kk, ii, fl: (0, 0, 0)),
                  kv_spec, kv_spec],
        out_specs=q_spec,
        scratch_shapes=[pltpu.VMEM((GT, TK), F32), pltpu.VMEM((GT, TK), F32),
                        pltpu.VMEM((GT, HEAD_DIM), F32)],
    )
    return pl.pallas_call(
        _nsa_selected_prompt_kernel,
        grid_spec=grid_spec,
        out_shape=jax.ShapeDtypeStruct((b, s, Q_DIM), F32),
        compiler_params=_cparams(("parallel", "parallel", "parallel")),
        name="nsa_selected_prompt",
    )(flags, slopes, q, sel, expand, kt, vt)


def _compress_prompt_kernel(k_ref, v_ref, wk_ref, wv_ref, e_ref, et_ref, kc_ref, kct_ref, vc_ref):
    nb = e_ref.shape[1]
    n_chunks = e_ref.shape[0]
    kc = jnp.zeros((nb, KV_DIM), F32)
    kct = jnp.zeros((KV_DIM, nb), F32)
    vc = jnp.zeros((nb, KV_DIM), F32)
    for c in range(n_chunks):
        khi, klo = _split_bf16(k_ref[0, :, c * LANES:(c + 1) * LANES] * wk_ref[...])
        vhi, vlo = _split_bf16(v_ref[0, :, c * LANES:(c + 1) * LANES] * wv_ref[...])
        kc = kc + _nt_dot(e_ref[c], khi) + _nt_dot(e_ref[c], klo)
        kct = kct + _dot(khi, et_ref[c]) + _dot(klo, et_ref[c])
        vc = vc + _nt_dot(e_ref[c], vhi) + _nt_dot(e_ref[c], vlo)
    for k in range(N_KV_HEADS):
        kc_ref[0, k] = kc[:, k * HEAD_DIM:(k + 1) * HEAD_DIM]
        kct_ref[0, k] = kct[k * HEAD_DIM:(k + 1) * HEAD_DIM, :]
        vc_ref[0, k] = vc[:, k * HEAD_DIM:(k + 1) * HEAD_DIM]


def compress_prompt(kct_stream, vct_stream, wk_t, wv_t, expand, expand_t):
    b, _, s = kct_stream.shape
    nb = s // NSA_BLOCK
    x_spec = pl.BlockSpec((1, KV_DIM, s), lambda bb: (bb, 0, 0))
    w_spec = pl.BlockSpec((KV_DIM, LANES), lambda bb: (0, 0))
    full = lambda a: pl.BlockSpec(a.shape, lambda bb: (0,) * a.ndim)
    row_major = pl.BlockSpec((1, N_KV_HEADS, nb, HEAD_DIM), lambda bb: (bb, 0, 0, 0))
    col_major = pl.BlockSpec((1, N_KV_HEADS, HEAD_DIM, nb), lambda bb: (bb, 0, 0, 0))
    return pl.pallas_call(
        _compress_prompt_kernel,
        grid=(b,),
        in_specs=[x_spec, x_spec, w_spec, w_spec, full(expand), full(expand_t)],
        out_specs=[row_major, col_major, row_major],
        out_shape=[jax.ShapeDtypeStruct((b, N_KV_HEADS, nb, HEAD_DIM), F32),
                   jax.ShapeDtypeStruct((b, N_KV_HEADS, HEAD_DIM, nb), F32),
                   jax.ShapeDtypeStruct((b, N_KV_HEADS, nb, HEAD_DIM), F32)],
        compiler_params=_cparams(("parallel",)),
        name="compress_prompt",
    )(kct_stream, vct_stream, wk_t, wv_t, expand, expand_t)


def _compress_sample_kernel(n_pages_step, pt_ref, wk_ref, wv_ref, *rest):
    k_refs = rest[:n_pages_step]
    v_refs = rest[n_pages_step:2 * n_pages_step]
    ok_ref, ov_ref = rest[2 * n_pages_step:]
    pp = pl.program_id(1)
    page = k_refs[0].shape[2]
    nb = ok_ref.shape[2]
    per_page = page // NSA_BLOCK

    @pl.when(pp == 0)
    def _():
        ok_ref[...] = jnp.zeros_like(ok_ref)
        ov_ref[...] = jnp.zeros_like(ov_ref)

    blk_of_row = lax.broadcasted_iota(I32, (page, nb), 0) >> NSA_BLOCK_SHIFT
    col = lax.broadcasted_iota(I32, (page, nb), 1)
    kc = jnp.zeros((KV_DIM, nb), F32)
    vc = jnp.zeros((KV_DIM, nb), F32)
    for i in range(n_pages_step):
        first_blk = (pp * n_pages_step + i) * per_page
        place = jnp.where(col == first_blk + blk_of_row, 1.0, 0.0).astype(BF16)
        khi, klo = _split_bf16(k_refs[i][0] * wk_ref[...])
        vhi, vlo = _split_bf16(v_refs[i][0] * wv_ref[...])
        kc = kc + _dot(khi, place) + _dot(klo, place)
        vc = vc + _dot(vhi, place) + _dot(vlo, place)
    ok_ref[0] += kc
    ov_ref[0] += vc


def compress_sample(pool_k, pool_v, page_base, page_table, wk_t, wv_t):
    b, n_pages = page_table.shape
    page = pool_k.shape[2]
    nb = n_pages * page // NSA_BLOCK
    per_step = _largest_divisor(n_pages, COMPRESS_PAGES_PER_STEP)

    def page_spec(i):
        return pl.BlockSpec((1, KV_DIM, page),
                            lambda bb, pp, pt: (page_base + pt[bb, pp * per_step + i], 0, 0))

    pages = [page_spec(i) for i in range(per_step)]
    w_spec = pl.BlockSpec((KV_DIM, page), lambda bb, pp, pt: (0, 0))
    o_spec = pl.BlockSpec((1, KV_DIM, nb), lambda bb, pp, pt: (bb, 0, 0))
    shape = jax.ShapeDtypeStruct((b, KV_DIM, nb), F32)
    grid_spec = pltpu.PrefetchScalarGridSpec(
        num_scalar_prefetch=1,
        grid=(b, n_pages // per_step),
        in_specs=[w_spec, w_spec] + pages + pages,
        out_specs=[o_spec, o_spec],
    )
    return pl.pallas_call(
        functools.partial(_compress_sample_kernel, per_step),
        grid_spec=grid_spec,
        out_shape=[shape, shape],
        compiler_params=_cparams(("parallel", "arbitrary")),
        name="compress_sample",
    )(page_table, wk_t, wv_t, *([pool_k] * per_step), *([pool_v] * per_step))


def _rank_rows(score_ref):
    n = score_ref.shape[0]
    score = score_ref[...]
    idx = lax.broadcasted_iota(I32, score.shape, 0)
    rank = jnp.zeros(score.shape, F32)
    for m in range(n):
        row_m = score_ref[m:m + 1, :]
        beats = jnp.where(row_m > score, 1.0, jnp.where(row_m == score, jnp.where(idx > m, 1.0, 0.0), 0.0))
        rank = rank + beats
    return rank


def _nsa_cmp_prompt_kernel(slopes_ref, q_ref, kc_ref, kct_ref, vc_ref, eye_ref, o_ref, sel_ref, score_sc):
    kk = pl.program_id(1)
    i = pl.program_id(2)
    nb = kc_ref.shape[2]
    q = _stack_heads(q_ref)
    s = _dot(q, kct_ref[0, 0].astype(BF16))
    q_pos = i * TQ + (lax.broadcasted_iota(I32, (GT, nb), 0) & (TQ - 1))
    dist = q_pos - (lax.broadcasted_iota(I32, (GT, nb), 1) * NSA_BLOCK + NSA_BLOCK - 1)
    valid = dist >= 0
    s = jnp.where(valid, s - _kv_slopes(slopes_ref, kk, (GT, nb), 0) * dist.astype(F32), NEG_INF)
    p = jnp.where(valid, jnp.exp(s - jnp.max(s, axis=1, keepdims=True)), 0.0)
    p = p / jnp.maximum(jnp.sum(p, axis=1, keepdims=True), TINY)
    o_ref[0] = _unstack_heads(_dot(p.astype(BF16), vc_ref[0, 0].astype(BF16)))
    st = _nt_dot(kc_ref[0, 0].astype(BF16), q)
    blk = lax.broadcasted_iota(I32, (nb, GT), 0)
    q_pos_t = i * TQ + (lax.broadcasted_iota(I32, (nb, GT), 1) & (TQ - 1))
    dist_t = q_pos_t - (blk * NSA_BLOCK + NSA_BLOCK - 1)
    valid_t = dist_t >= 0
    st = jnp.where(valid_t, st - _kv_slopes(slopes_ref, kk, (nb, GT), 1) * dist_t.astype(F32), NEG_INF)
    pt = jnp.where(valid_t, jnp.exp(st - jnp.max(st, axis=0, keepdims=True)), 0.0)
    pt = pt / jnp.maximum(jnp.sum(pt, axis=0, keepdims=True), TINY)
    imp = pt[:, 0:TQ]
    for g in range(1, GROUP):
        imp = imp + pt[:, g * TQ:(g + 1) * TQ]
    blk_q = lax.broadcasted_iota(I32, (nb, TQ), 0)
    cur = (i * TQ + lax.broadcasted_iota(I32, (nb, TQ), 1)) >> NSA_BLOCK_SHIFT
    score_sc[...] = jnp.where(blk_q == cur, FORCE_SCORE, jnp.where(blk_q < cur, imp, -1.0))
    rank = _rank_rows(score_sc)
    sel_t = jnp.where(rank < float(min(NSA_TOPK, nb)), 1.0, 0.0).astype(BF16)
    sel_ref[0, 0] = _nt_dot(eye_ref[...], sel_t).astype(BF16)


def nsa_cmp_prompt(q, kcmp, kcmp_t, vcmp, eye, slopes):
    b, s, _ = q.shape
    nb = kcmp.shape[2]
    q_spec = pl.BlockSpec((1, TQ, KV_DIM), lambda bb, kk, ii: (bb, ii, kk))
    row_major = pl.BlockSpec((1, 1, nb, HEAD_DIM), lambda bb, kk, ii: (bb, kk, 0, 0))
    col_major = pl.BlockSpec((1, 1, HEAD_DIM, nb), lambda bb, kk, ii: (bb, kk, 0, 0))
    return pl.pallas_call(
        _nsa_cmp_prompt_kernel,
        grid=(b, N_KV_HEADS, s // TQ),
        in_specs=[pl.BlockSpec(memory_space=pltpu.SMEM), q_spec, row_major, col_major, row_major,
                  pl.BlockSpec((TQ, TQ), lambda bb, kk, ii: (0, 0))],
        out_specs=[q_spec, pl.BlockSpec((1, 1, TQ, nb), lambda bb, kk, ii: (bb, kk, ii, 0))],
        out_shape=[jax.ShapeDtypeStruct((b, s, Q_DIM), F32),
                   jax.ShapeDtypeStruct((b, N_KV_HEADS, s, nb), BF16)],
        scratch_shapes=[pltpu.VMEM((nb, TQ), F32)],
        compiler_params=_cparams(("parallel", "parallel", "parallel")),
        name="nsa_cmp_prompt",
    )(slopes, q, kcmp, kcmp_t, vcmp, eye)


def _nsa_cmp_sample_kernel(past_len, q_ref, slope_ref, gsum_ref, kc_ref, vc_ref, o_ref, idx_ref):
    nb = kc_ref.shape[2]
    s = _dot(q_ref[0], kc_ref[0].astype(BF16))
    blk = lax.broadcasted_iota(I32, (N_HEADS, nb), 1)
    dist = past_len - (blk * NSA_BLOCK + NSA_BLOCK - 1)
    valid = dist >= 0
    s = jnp.where(valid, s - slope_ref[...][:, :1] * dist.astype(F32), NEG_INF)
    m = jnp.max(s, axis=1, keepdims=True)
    p = jnp.where(valid, jnp.exp(s - m), 0.0)
    p = p / jnp.maximum(jnp.sum(p, axis=1, keepdims=True), TINY)
    o_ref[0] = _fold_heads(_nt_dot(p.astype(BF16), vc_ref[0].astype(BF16)))
    imp = jnp.dot(gsum_ref[...], p, preferred_element_type=F32, precision=lax.Precision.HIGHEST)
    rows = imp.shape[0]
    blk8 = lax.broadcasted_iota(I32, (rows, nb), 1)
    rank = jnp.zeros((rows, nb), F32)
    for mcol in range(nb):
        col_m = imp[:, mcol:mcol + 1]
        rank = rank + jnp.where(col_m > imp, 1.0,
                                jnp.where(col_m == imp, jnp.where(blk8 > mcol, 1.0, 0.0), 0.0))
    blk_f = blk8.astype(F32)
    out = jnp.zeros((rows, nb), F32)
    for r in range(NSA_TOPK - 1):
        idx_r = jnp.sum(jnp.where(rank == float(r), blk_f, 0.0), axis=1, keepdims=True)
        out = jnp.where(blk8 == r, idx_r, out)
    idx_ref[0] = out.astype(I32)


def nsa_cmp_sample(qbd, slope_rows, gsum, kcmp_t, vcmp_t, past_len):
    b = qbd.shape[0]
    nb = kcmp_t.shape[2]
    rows = gsum.shape[0]
    return pl.pallas_call(
        functools.partial(_nsa_cmp_sample_kernel, past_len),
        grid=(b,),
        in_specs=[pl.BlockSpec((1, N_HEADS, KV_DIM), lambda bb: (bb, 0, 0)),
                  pl.BlockSpec((N_HEADS, LANES), lambda bb: (0, 0)),
                  pl.BlockSpec((rows, N_HEADS), lambda bb: (0, 0)),
                  pl.BlockSpec((1, KV_DIM, nb), lambda bb: (bb, 0, 0)),
                  pl.BlockSpec((1, KV_DIM, nb), lambda bb: (bb, 0, 0))],
        out_specs=[pl.BlockSpec((1, N_HEADS, HEAD_DIM), lambda bb: (bb, 0, 0)),
                   pl.BlockSpec((1, rows, nb), lambda bb: (bb, 0, 0))],
        out_shape=[jax.ShapeDtypeStruct((b, N_HEADS, HEAD_DIM), F32),
                   jax.ShapeDtypeStruct((b, rows, nb), I32)],
        compiler_params=_cparams(("parallel",)),
        name="nsa_cmp_sample",
    )(qbd, slope_rows, gsum, kcmp_t, vcmp_t)


def _nsa_selected_sample_kernel(past_len, phys_ref, logi_ref, q_ref, slope_ref, nk_ref, nv_ref, *rest):
    n_past = NSA_TOPK - 1
    k_refs = rest[:n_past]
    v_refs = rest[n_past:2 * n_past]
    o_ref = rest[2 * n_past]
    page = k_refs[0].shape[2]
    per_page = page // NSA_BLOCK
    base = (pl.program_id(0) * N_KV_HEADS + pl.program_id(1)) * n_past
    q = q_ref[0, 0]
    slope = slope_ref[0]
    lane = lax.broadcasted_iota(I32, (N_HEADS, page), 1)
    scores, values = [], []
    for j in range(n_past):
        blk = logi_ref[base + j]
        first = (blk // per_page) * page
        dist = past_len - (first + lane)
        valid = jnp.logical_and((lane >> NSA_BLOCK_SHIFT) == blk % per_page, dist >= 0)
        z = _dot(q, k_refs[j][0].astype(BF16))
        scores.append(jnp.where(valid, z - slope[:, :1] * dist.astype(F32), NEG_INF))
        values.append(v_refs[j][0].astype(BF16))
    n_new = nk_ref.shape[2]
    is_new = lax.broadcasted_iota(I32, (N_HEADS, n_new), 1) == 0
    s_new = jnp.where(is_new, _nt_dot(q, nk_ref[0, 0].astype(BF16)), NEG_INF)
    m = jnp.max(s_new, axis=1, keepdims=True)
    for sc in scores:
        m = jnp.maximum(m, jnp.max(sc, axis=1, keepdims=True))
    p_new = jnp.exp(s_new - m)
    denom = jnp.sum(p_new, axis=1, keepdims=True)
    acc = _dot(p_new.astype(BF16), nv_ref[0, 0].astype(BF16))
    for sc, val in zip(scores, values):
        p = jnp.exp(sc - m)
        denom = denom + jnp.sum(p, axis=1, keepdims=True)
        acc = acc + _nt_dot(p.astype(BF16), val)
    o_ref[0, 0] = acc / denom


def nsa_selected_sample(q4, slope4, phys_idx, logi_idx, pool_k, pool_v, new_k, new_v, past_len):
    b = q4.shape[0]
    n_past = NSA_TOPK - 1
    page = pool_k.shape[2]

    def blk_spec(j):
        return pl.BlockSpec(
            (1, HEAD_DIM, page),
            lambda bb, kk, phys, logi: (phys[(bb * N_KV_HEADS + kk) * n_past + j], 0, 0))

    blks = [blk_spec(j) for j in range(n_past)]
    per_head = lambda shape: pl.BlockSpec((1, 1) + shape, lambda bb, kk, phys, logi: (bb, kk, 0, 0))
    grid_spec = pltpu.PrefetchScalarGridSpec(
        num_scalar_prefetch=2,
        grid=(b, N_KV_HEADS),
        in_specs=[per_head((N_HEADS, HEAD_DIM)),
                  pl.BlockSpec((1, N_HEADS, LANES), lambda bb, kk, phys, logi: (kk, 0, 0)),
                  per_head((NEW_ROWS, HEAD_DIM)), per_head((NEW_ROWS, HEAD_DIM))] + blks + blks,
        out_specs=per_head((N_HEADS, HEAD_DIM)),
    )
    return pl.pallas_call(
        functools.partial(_nsa_selected_sample_kernel, past_len),
        grid_spec=grid_spec,
        out_shape=jax.ShapeDtypeStruct((b, N_KV_HEADS, N_HEADS, HEAD_DIM), F32),
        compiler_params=_cparams(("parallel", "parallel")),
        name="nsa_selected_sample",
    )(phys_idx, logi_idx, q4, slope4, new_k, new_v, *([pool_k] * n_past), *([pool_v] * n_past))


def _window_sample_kernel(window, has_sink, q_ref, slope_ref, sink_ref, wk_ref, wv_ref,
                          nk_ref, nv_ref, o_ref):
    n_buf = wk_ref.shape[2]
    n_new = nk_ref.shape[1]
    q = q_ref[0]
    dist1 = n_buf - lax.broadcasted_iota(I32, (N_HEADS, n_buf), 1)
    valid1 = dist1 <= window
    s1 = jnp.where(valid1,
                   _dot(q, wk_ref[0].astype(BF16)) - slope_ref[...][:, :1] * dist1.astype(F32),
                   NEG_INF)
    valid2 = lax.broadcasted_iota(I32, (N_HEADS, n_new), 1) == 0
    s2 = jnp.where(valid2, _nt_dot(q, nk_ref[0].astype(BF16)), NEG_INF)
    m = jnp.maximum(jnp.max(s1, axis=1, keepdims=True), jnp.max(s2, axis=1, keepdims=True))
    if has_sink:
        sink = sink_ref[...][:, :1]
        m = jnp.maximum(m, sink)
    p1 = jnp.where(valid1, jnp.exp(s1 - m), 0.0)
    p2 = jnp.where(valid2, jnp.exp(s2 - m), 0.0)
    denom = jnp.sum(p1, axis=1, keepdims=True) + jnp.sum(p2, axis=1, keepdims=True)
    if has_sink:
        denom = denom + jnp.exp(sink - m)
    acc = _nt_dot(p1.astype(BF16), wv_ref[0].astype(BF16)) + _dot(p2.astype(BF16), nv_ref[0].astype(BF16))
    o_ref[0] = _fold_heads(acc / denom)


def window_sample_attention(qbd, slope_rows, sink_rows, win_kt, win_vt, new_k, new_v, window, has_sink):
    b, _, n_buf = win_kt.shape
    n_new = new_k.shape[1]
    const = pl.BlockSpec((N_HEADS, LANES), lambda bb: (0, 0))
    return pl.pallas_call(
        functools.partial(_window_sample_kernel, window, has_sink),
        grid=(b,),
        in_specs=[pl.BlockSpec((1, N_HEADS, KV_DIM), lambda bb: (bb, 0, 0)), const, const,
                  pl.BlockSpec((1, KV_DIM, n_buf), lambda bb: (bb, 0, 0)),
                  pl.BlockSpec((1, KV_DIM, n_buf), lambda bb: (bb, 0, 0)),
                  pl.BlockSpec((1, n_new, KV_DIM), lambda bb: (bb, 0, 0)),
                  pl.BlockSpec((1, n_new, KV_DIM), lambda bb: (bb, 0, 0))],
        out_specs=pl.BlockSpec((1, N_HEADS, HEAD_DIM), lambda bb: (bb, 0, 0)),
        out_shape=jax.ShapeDtypeStruct((b, N_HEADS, HEAD_DIM), F32),
        compiler_params=_cparams(("parallel",)),
        name="window_sample",
    )(qbd, slope_rows, sink_rows, win_kt, win_vt, new_k, new_v)


def _block_diag_queries(q):
    b = q.shape[0]
    qh = q.reshape(b, N_KV_HEADS, GROUP, 1, HEAD_DIM)
    eye = jnp.eye(N_KV_HEADS, dtype=q.dtype).reshape(N_KV_HEADS, 1, N_KV_HEADS, 1)
    return (qh * eye).reshape(b, N_HEADS, KV_DIM)


def _suffix_matrix(n):
    r = jnp.arange(n)
    strict = (r[:, None] > r[None, :]).astype(BF16)
    return jnp.concatenate([strict, jnp.ones((n, n), BF16)], axis=1)


def _new_token_rows(x):
    return jnp.pad(x[:, None, :], ((0, 0), (0, NEW_ROWS - 1), (0, 0)))


def _paged_view(pool):
    n_layers, n_pool, page = pool.shape[:3]
    return pool.transpose(0, 1, 3, 4, 2).reshape(n_layers * n_pool, KV_DIM, page)


def _window_view(cache):
    b, n = cache.shape[:2]
    return cache.transpose(0, 2, 3, 1).reshape(b, KV_DIM, n)


def _stream_leaf(t):
    b, _, n = t.shape
    return t.reshape(b, N_KV_HEADS, HEAD_DIM, n).transpose(0, 3, 1, 2)


def _position_weights(w, reps):
    return jnp.tile(jnp.repeat(w.T, HEAD_DIM, axis=0), (1, reps))


def kernel(x_prompt, x_sample, cache_sb_k, cache_sb_v, cache_nsa_cmp_k, cache_nsa_cmp_v,
           cache_nsa_sel_k, cache_nsa_sel_v, cache_nsa_win_k, cache_nsa_win_v, cache_swa_k,
           cache_swa_v, page_table, norm_mix, norm_ffn, norm_final, w_sb_qkv, w_sb_o, w_nsa_in,
           w_nsa_o, nsa_cmp_wk, nsa_cmp_wv, w_swa_qkv, w_swa_o, swa_sinks, w_ffn_up, w_ffn_down):
    b, s, d = x_prompt.shape
    db = x_sample.shape[0]
    depth = norm_mix.shape[0]
    n_pool, page = cache_sb_k.shape[1:3]
    n_pages = page_table.shape[1]
    past_len = n_pages * page
    nsa_buf = cache_nsa_win_k.shape[2]
    swa_buf = cache_swa_k.shape[2]
    nq = s // TQ

    heads = jnp.arange(1, N_HEADS + 1, dtype=F32)
    slopes = jnp.exp2(-8.0 * heads / N_HEADS)
    slope_rows = jnp.broadcast_to(slopes[:, None], (N_HEADS, LANES))
    uo = _suffix_matrix(TK)
    no_sinks = jnp.zeros((N_HEADS,), F32)

    xp = x_prompt.reshape(b * s, d)
    xs = x_sample.reshape(db, d)
    tm_p, tm_s = 256, db

    def split_weights(w, n_t):
        wq = w[:, :Q_DIM].astype(BF16)
        wt = w[:, Q_DIM:Q_DIM + n_t * KV_DIM].T.astype(BF16)
        return wq, wt

    def project(x_p, x_s, g, wq, wt, wg):
        outs_p = norm_proj(x_p, g, wq, wt, wg, b, tm_p)
        outs_s = norm_proj(x_s, g, wq, wt, wg, 1, tm_s)
        return outs_p, outs_s

    def sample_rows(t):
        return t[0].T

    outs = {name: [] for name in (
        "sb_k_p", "sb_v_p", "sb_k_s", "sb_v_s", "cmp_k_p", "cmp_v_p", "cmp_k_s", "cmp_v_s",
        "sel_k_p", "sel_v_p", "sel_k_s", "sel_v_s", "win_k_p", "win_v_p", "win_k_s", "win_v_s",
        "swa_k_p", "swa_v_p", "swa_k_s", "swa_v_s")}

    def new_leaf(rows):
        return rows.reshape(db, 1, N_KV_HEADS, HEAD_DIM)

    def shifted_window(win_t, new_rows):
        return _stream_leaf(jnp.concatenate([win_t[:, :, 1:], new_rows[:, :, None]], axis=2))

    ia = ib = ic = 0
    for i in range(depth):
        kind = i % N_MIXERS
        if kind == 0:
            wq, wt = split_weights(w_sb_qkv[ia], 2)
            w_o = w_sb_o[ia].astype(BF16)
            (q_p, kt, vt), (q_s, kt_s, vt_s) = project(xp, xs, norm_mix[i], wq, wt, None)
            o_p = sb_prompt_attention(q_p.reshape(b, s, Q_DIM), kt.reshape(b, N_KV_HEADS, HEAD_DIM, s),
                                      vt.reshape(b, N_KV_HEADS, HEAD_DIM, s), uo)
            o_s = sb_sample_attention(_block_diag_queries(q_s), _paged_view(cache_sb_k),
                                      _paged_view(cache_sb_v), ia * n_pool, page_table, uo)
            xp = out_proj(xp, o_p.reshape(b * s, Q_DIM), w_o, tm_p)
            xs = out_proj(xs, o_s.reshape(db, Q_DIM), w_o, tm_s)
            outs["sb_k_p"].append(_stream_leaf(kt))
            outs["sb_v_p"].append(_stream_leaf(vt))
            outs["sb_k_s"].append(new_leaf(sample_rows(kt_s)))
            outs["sb_v_s"].append(new_leaf(sample_rows(vt_s)))
            ia += 1
        elif kind == 1:
            wq, wt = split_weights(w_nsa_in[ib], 6)
            n_gate = NSA_GATES * N_HEADS
            wg = jnp.pad(w_nsa_in[ib][:, Q_DIM + 6 * KV_DIM:], ((0, 0), (0, LANES - n_gate))).astype(BF16)
            w_o = w_nsa_o[ib].astype(BF16)
            gate_expand = (jnp.arange(LANES)[None, :, None]
                           == (jnp.arange(Q_DIM)[None, None, :] // HEAD_DIM) * NSA_GATES
                           + jnp.arange(NSA_GATES)[:, None, None]).astype(BF16)
            outs_p, outs_s = project(xp, xs, norm_mix[i], wq, wt, wg)

            q_p, kc, vc, ks, vs, kw, vw, gl_p = outs_p
            q3 = q_p.reshape(b, s, Q_DIM)
            heads_view = lambda t: t.reshape(b, N_KV_HEADS, HEAD_DIM, s)
            nb = s // NSA_BLOCK
            per_tile = TK // NSA_BLOCK
            expand = (jnp.arange(nb)[None, :, None]
                      == (jnp.arange(s // TK)[:, None, None] * per_tile
                          + jnp.arange(TK)[None, None, :] // NSA_BLOCK)).astype(BF16)
            kcmp, kcmp_t, vcmp = compress_prompt(kc, vc, _position_weights(nsa_cmp_wk[ib], LANES // NSA_BLOCK),
                                                 _position_weights(nsa_cmp_wv[ib], LANES // NSA_BLOCK),
                                                 expand, expand.swapaxes(1, 2))
            o_cmp, sel = nsa_cmp_prompt(q3, kcmp, kcmp_t, vcmp, jnp.eye(TQ, dtype=BF16), slopes)
            flags = sel.reshape(b, N_KV_HEADS, nq, TQ, s // TK, per_tile).max(axis=(3, 5))
            flags = jnp.sum((flags > 0).astype(I32) << jnp.arange(s // TK, dtype=I32), axis=-1).reshape(-1)
            o_sel = nsa_selected_prompt(q3, sel, flags, expand, heads_view(ks), heads_view(vs), slopes)
            o_win = banded_attention(q3, heads_view(kw), heads_view(vw), slopes, no_sinks,
                                     NSA_WINDOW, False, F32)
            xp = nsa_out_proj(xp, gl_p, gate_expand,
                              [o.reshape(b * s, Q_DIM) for o in (o_cmp, o_sel, o_win)], w_o, tm_p)
            for name, val in (("cmp_k_p", kc), ("cmp_v_p", vc), ("sel_k_p", ks), ("sel_v_p", vs)):
                outs[name].append(_stream_leaf(val))
            outs["win_k_p"].append(_stream_leaf(kw[:, :, s - nsa_buf:]))
            outs["win_v_p"].append(_stream_leaf(vw[:, :, s - nsa_buf:]))

            q_s, kc, vc, ks, vs, kw, vw, gl_s = outs_s
            kc, vc, ks, vs, kw, vw = [sample_rows(t) for t in (kc, vc, ks, vs, kw, vw)]
            qbd = _block_diag_queries(q_s)
            kcmp_t, vcmp_t = compress_sample(_paged_view(cache_nsa_cmp_k), _paged_view(cache_nsa_cmp_v),
                                             ib * n_pool, page_table,
                                             _position_weights(nsa_cmp_wk[ib], page // NSA_BLOCK),
                                             _position_weights(nsa_cmp_wv[ib], page // NSA_BLOCK))
            gsum = (jnp.arange(8)[:, None] == jnp.arange(N_HEADS)[None, :] // GROUP).astype(F32)
            o_cmp, idx = nsa_cmp_sample(qbd, slope_rows, gsum, kcmp_t, vcmp_t, past_len)
            idx = idx[:, :N_KV_HEADS, :NSA_TOPK - 1]
            per_page = page // NSA_BLOCK
            pages = jnp.take_along_axis(page_table[:, None, :], idx // per_page, axis=2)
            phys = ((ib * n_pool + pages) * N_KV_HEADS + jnp.arange(N_KV_HEADS)[None, :, None])
            head_pages = lambda pool: _paged_view(pool).reshape(-1, HEAD_DIM, page)
            q4 = jnp.pad(q_s.reshape(db, N_KV_HEADS, GROUP, HEAD_DIM),
                         ((0, 0), (0, 0), (0, N_HEADS - GROUP), (0, 0)))
            slope4 = jnp.pad(jnp.broadcast_to(slopes.reshape(N_KV_HEADS, GROUP, 1),
                                              (N_KV_HEADS, GROUP, LANES)),
                             ((0, 0), (0, N_HEADS - GROUP), (0, 0)))
            new_head_rows = lambda x: jnp.pad(x.reshape(db, N_KV_HEADS, 1, HEAD_DIM),
                                              ((0, 0), (0, 0), (0, NEW_ROWS - 1), (0, 0)))
            o_sel = nsa_selected_sample(q4, slope4, phys.reshape(-1).astype(I32), idx.reshape(-1),
                                        head_pages(cache_nsa_sel_k), head_pages(cache_nsa_sel_v),
                                        new_head_rows(ks), new_head_rows(vs), past_len)
            o_sel = o_sel[:, :, :GROUP]
            win_kt = _window_view(cache_nsa_win_k[ib])
            win_vt = _window_view(cache_nsa_win_v[ib])
            o_win = window_sample_attention(qbd, slope_rows, slope_rows, win_kt, win_vt,
                                            _new_token_rows(kw), _new_token_rows(vw), NSA_WINDOW, False)
            xs = nsa_out_proj(xs, gl_s, gate_expand,
                              [o.reshape(db, Q_DIM) for o in (o_cmp, o_sel, o_win)], w_o, tm_s)
            for name, val in (("cmp_k_s", kc), ("cmp_v_s", vc), ("sel_k_s", ks), ("sel_v_s", vs)):
                outs[name].append(new_leaf(val))
            outs["win_k_s"].append(shifted_window(win_kt, kw))
            outs["win_v_s"].append(shifted_window(win_vt, vw))
            ib += 1
        else:
            wq, wt = split_weights(w_swa_qkv[ic], 2)
            w_o = w_swa_o[ic].astype(BF16)
            sinks = swa_sinks[ic]
            (q_p, kt, vt), (q_s, kt_s, vt_s) = project(xp, xs, norm_mix[i], wq, wt, None)
            o_p = banded_attention(q_p.reshape(b, s, Q_DIM), kt.reshape(b, N_KV_HEADS, HEAD_DIM, s),
                                   vt.reshape(b, N_KV_HEADS, HEAD_DIM, s), slopes, sinks,
                                   SWA_WINDOW, True, BF16)
            k_new, v_new = sample_rows(kt_s), sample_rows(vt_s)
            buf_kt = _window_view(cache_swa_k[ic])
            buf_vt = _window_view(cache_swa_v[ic])
            sink_rows = jnp.broadcast_to(sinks[:, None], (N_HEADS, LANES))
            o_s = window_sample_attention(_block_diag_queries(q_s), slope_rows, sink_rows, buf_kt, buf_vt,
                                          _new_token_rows(k_new), _new_token_rows(v_new), SWA_WINDOW, True)
            xp = out_proj(xp, o_p.reshape(b * s, Q_DIM), w_o, tm_p)
            xs = out_proj(xs, o_s.reshape(db, Q_DIM), w_o, tm_s)
            outs["swa_k_p"].append(_stream_leaf(kt[:, :, s - swa_buf:]))
            outs["swa_v_p"].append(_stream_leaf(vt[:, :, s - swa_buf:]))
            outs["swa_k_s"].append(shifted_window(buf_kt, k_new))
            outs["swa_v_s"].append(shifted_window(buf_vt, v_new))
            ic += 1
        w_up = w_ffn_up[i].astype(BF16)
        w_down = w_ffn_down[i].astype(BF16)
        xp = ffn(xp, norm_ffn[i], w_up, w_down, 512, 512)
        xs = ffn(xs, norm_ffn[i], w_up, w_down, tm_s, 512)

    y_prompt = final_norm(xp, norm_final, tm_p).reshape(b, s, d)
    y_sample = final_norm(xs, norm_final, tm_s).reshape(db, 1, d)
    st = {name: jnp.stack(vals) for name, vals in outs.items()}
    return (y_prompt, y_sample,
            st["sb_k_p"], st["sb_v_p"], st["sb_k_s"], st["sb_v_s"],
            st["cmp_k_p"], st["cmp_v_p"], st["cmp_k_s"], st["cmp_v_s"],
            st["sel_k_p"], st["sel_v_p"], st["sel_k_s"], st["sel_v_s"],
            st["win_k_p"], st["win_v_p"], st["win_k_s"], st["win_v_s"],
            st["swa_k_p"], st["swa_v_p"], st["swa_k_s"], st["swa_v_s"])
```

```python
import functools
import math

import jax
import jax.numpy as jnp
from jax import lax
from jax.experimental import pallas as pl
from jax.experimental.pallas import tpu as pltpu

F32 = jnp.float32
BF16 = jnp.bfloat16
I32 = jnp.int32

D_MODEL = 1024
HEAD_DIM = 64
N_HEADS = 16
N_KV_HEADS = 4
GROUP = 4
Q_DIM = N_HEADS * HEAD_DIM
KV_DIM = N_KV_HEADS * HEAD_DIM
N_MIXERS = 3
NSA_BLOCK = 64
NSA_TOPK = 16
NSA_WINDOW = 512
SWA_WINDOW = 128
NSA_GATES = 3
ATTN_SCALE = HEAD_DIM ** -0.5
RMS_EPS = 1e-6
NEG_INF = -1e30
TINY = 1e-30
FORCE_SCORE = float(GROUP + 1)

LANES = 128
TQ = 128
TK = 128
GT = GROUP * TQ
GROUP_SHIFT = GROUP.bit_length() - 1
TQ_SHIFT = TQ.bit_length() - 1
NSA_BLOCK_SHIFT = NSA_BLOCK.bit_length() - 1
N_SLOPE_TERMS = 3
SB_HEAD_PAGES = 4
SB_TAIL_PAGES_PER_STEP = 15
COMPRESS_PAGES_PER_STEP = 8
NEW_ROWS = 8
SB_DEAD_LOG = -104.0
VMEM_LIMIT = 48 * 1024 * 1024


def _cparams(sem):
    return pltpu.CompilerParams(dimension_semantics=sem, vmem_limit_bytes=VMEM_LIMIT)


def _nt_dot(a, b):
    return lax.dot_general(a, b, (((1,), (1,)), ((), ())), preferred_element_type=F32)


def _dot(a, b):
    return jnp.dot(a, b, preferred_element_type=F32)


def _split_bf16(x):
    hi = x.astype(BF16)
    return hi, (x - hi.astype(F32)).astype(BF16)


def _rms(x, g):
    ms = jnp.mean(x * x, axis=-1, keepdims=True)
    return (x * lax.rsqrt(ms + RMS_EPS)) * g


def _largest_divisor(n, cap):
    return max(d for d in range(1, cap + 1) if n % d == 0)


def _proj_kernel(n_t, n_aug, has_gate, x_ref, g_ref, wqt_ref, wt_ref, *rest):
    rest = list(rest)
    wa_ref = rest.pop(0) if n_aug else None
    wg_ref = rest.pop(0) if has_gate else None
    qt_ref, t_refs, a_refs = rest[0], rest[1:1 + n_t], rest[1 + n_t:1 + n_t + n_aug]
    h = _rms(x_ref[...], g_ref[...]).astype(BF16)
    qt_ref[0] = (_nt_dot(wqt_ref[...], h) * ATTN_SCALE).astype(BF16)
    for j in range(n_t):
        t_refs[j][0] = _nt_dot(wt_ref[j * KV_DIM:(j + 1) * KV_DIM, :], h)
    if n_aug:
        tm = h.shape[0]
        lane = lax.broadcasted_iota(I32, (tm, HEAD_DIM), 1)
        in_tile = lax.broadcasted_iota(I32, (tm, HEAD_DIM), 0) & (TK - 1)
        pos = jnp.where(lane < N_SLOPE_TERMS, in_tile, 0).astype(F32)
        for j in range(n_aug):
            std = _dot(h, wa_ref[:, j * KV_DIM:(j + 1) * KV_DIM])
            for k in range(N_KV_HEADS):
                a_refs[j][0, k] = jnp.concatenate(
                    [std[:, k * HEAD_DIM:(k + 1) * HEAD_DIM], pos], axis=1).astype(BF16)
    if has_gate:
        rest[1 + n_t + n_aug][...] = _dot(h, wg_ref[...])


def norm_proj(x, g, wqt, wt, wa, wg, b, tm):
    m, d = x.shape
    s = m // b
    per_b = s // tm
    n_t = wt.shape[0] // KV_DIM
    n_aug = 0 if wa is None else wa.shape[1] // KV_DIM
    has_gate = wg is not None
    assert tm % TK == 0 or n_aug == 0
    full = lambda a: pl.BlockSpec(a.shape, lambda i: (0,) * a.ndim)
    in_specs = [pl.BlockSpec((tm, d), lambda i: (i, 0)), pl.BlockSpec((1, d), lambda i: (0, 0)),
                full(wqt), full(wt)]
    args = [x, g.reshape(1, d), wqt, wt]
    if n_aug:
        in_specs.append(full(wa))
        args.append(wa)
    if has_gate:
        in_specs.append(full(wg))
        args.append(wg)
    out_specs = [pl.BlockSpec((1, Q_DIM, tm), lambda i: (i // per_b, 0, i % per_b))]
    out_shape = [jax.ShapeDtypeStruct((b, Q_DIM, s), BF16)]
    for _ in range(n_t):
        out_specs.append(pl.BlockSpec((1, KV_DIM, tm), lambda i: (i // per_b, 0, i % per_b)))
        out_shape.append(jax.ShapeDtypeStruct((b, KV_DIM, s), F32))
    for _ in range(n_aug):
        out_specs.append(pl.BlockSpec((1, N_KV_HEADS, tm, 2 * HEAD_DIM), lambda i: (i // per_b, 0, i % per_b, 0)))
        out_shape.append(jax.ShapeDtypeStruct((b, N_KV_HEADS, s, 2 * HEAD_DIM), BF16))
    if has_gate:
        out_specs.append(pl.BlockSpec((tm, LANES), lambda i: (i, 0)))
        out_shape.append(jax.ShapeDtypeStruct((m, LANES), F32))
    return pl.pallas_call(
        functools.partial(_proj_kernel, n_t, n_aug, has_gate),
        grid=(m // tm,),
        in_specs=in_specs,
        out_specs=out_specs,
        out_shape=out_shape,
        compiler_params=_cparams(("parallel",)),
        name="norm_proj",
    )(*args)


def _out_proj_kernel(res_ref, a_ref, w_ref, o_ref):
    o_ref[...] = res_ref[...] + _dot(a_ref[...].astype(BF16), w_ref[...])


def out_proj(res, a, w, tm):
    m, d = res.shape
    k = a.shape[1]
    return pl.pallas_call(
        _out_proj_kernel,
        grid=(m // tm,),
        in_specs=[pl.BlockSpec((tm, d), lambda i: (i, 0)),
                  pl.BlockSpec((tm, k), lambda i: (i, 0)),
                  pl.BlockSpec((k, d), lambda i: (0, 0))],
        out_specs=pl.BlockSpec((tm, d), lambda i: (i, 0)),
        out_shape=jax.ShapeDtypeStruct((m, d), F32),
        compiler_params=_cparams(("parallel",)),
        name="out_proj",
    )(res, a, w)


def _nsa_out_proj_kernel(res_ref, gl_ref, x_ref, a0_ref, a1_ref, a2_ref, w_ref, o_ref):
    hi, lo = _split_bf16(jax.nn.sigmoid(gl_ref[...]))
    o = jnp.zeros(res_ref.shape, F32)
    for j, a_ref in enumerate((a0_ref, a1_ref, a2_ref)):
        gate = _dot(hi, x_ref[j]) + _dot(lo, x_ref[j])
        o = o + gate * a_ref[...]
    o_ref[...] = res_ref[...] + _dot(o.astype(BF16), w_ref[...])


def nsa_out_proj(res, gate_logits, gate_expand, branches, w, tm):
    m, d = res.shape
    row = pl.BlockSpec((tm, d), lambda i: (i, 0))
    return pl.pallas_call(
        _nsa_out_proj_kernel,
        grid=(m // tm,),
        in_specs=[row, pl.BlockSpec((tm, LANES), lambda i: (i, 0)),
                  pl.BlockSpec(gate_expand.shape, lambda i: (0, 0, 0)), row, row, row,
                  pl.BlockSpec((d, d), lambda i: (0, 0))],
        out_specs=row,
        out_shape=jax.ShapeDtypeStruct((m, d), F32),
        compiler_params=_cparams(("parallel",)),
        name="nsa_out_proj",
    )(res, gate_logits, gate_expand, *branches, w)


def _ffn_kernel(x_ref, g_ref, wu_ref, wd_ref, o_ref, h_sc, acc_sc):
    f = pl.program_id(1)

    @pl.when(f == 0)
    def _():
        h_sc[...] = _rms(x_ref[...], g_ref[...]).astype(BF16)
        acc_sc[...] = jnp.zeros_like(acc_sc)

    u = jnp.maximum(_dot(h_sc[...], wu_ref[...]), 0.0)
    acc_sc[...] += _dot((u * u).astype(BF16), wd_ref[...])

    @pl.when(f == pl.num_programs(1) - 1)
    def _():
        o_ref[...] = x_ref[...] + acc_sc[...]


def ffn(x, g, w_up, w_down, tm, tf):
    m, d = x.shape
    dff = w_up.shape[1]
    return pl.pallas_call(
        _ffn_kernel,
        grid=(m // tm, dff // tf),
        in_specs=[pl.BlockSpec((tm, d), lambda i, f: (i, 0)),
                  pl.BlockSpec((1, d), lambda i, f: (0, 0)),
                  pl.BlockSpec((d, tf), lambda i, f: (0, f)),
                  pl.BlockSpec((tf, d), lambda i, f: (f, 0))],
        out_specs=pl.BlockSpec((tm, d), lambda i, f: (i, 0)),
        out_shape=jax.ShapeDtypeStruct((m, d), F32),
        scratch_shapes=[pltpu.VMEM((tm, d), BF16), pltpu.VMEM((tm, d), F32)],
        compiler_params=_cparams(("parallel", "arbitrary")),
        name="ffn",
    )(x, g.reshape(1, d), w_up, w_down)


def _final_norm_kernel(x_ref, g_ref, o_ref):
    o_ref[...] = _rms(x_ref[...], g_ref[...])


def final_norm(x, g, tm):
    m, d = x.shape
    return pl.pallas_call(
        _final_norm_kernel,
        grid=(m // tm,),
        in_specs=[pl.BlockSpec((tm, d), lambda i: (i, 0)), pl.BlockSpec((1, d), lambda i: (0, 0))],
        out_specs=pl.BlockSpec((tm, d), lambda i: (i, 0)),
        out_shape=jax.ShapeDtypeStruct((m, d), F32),
        compiler_params=_cparams(("parallel",)),
        name="final_norm",
    )(x, g.reshape(1, d))


def _stack_queries(qt_ref):
    q = qt_ref[0]
    return jnp.concatenate([q[g * HEAD_DIM:(g + 1) * HEAD_DIM, :] for g in range(GROUP)], axis=1)


def _untranspose(acc_t):
    return jnp.concatenate([acc_t[:, g * TQ:(g + 1) * TQ].T for g in range(GROUP)], axis=1)


def _key_tile(ref, j):
    return ref[0, 0, :, pl.ds(pl.multiple_of(j * TK, TK), TK)].astype(BF16)


def _key_rows(ref, j):
    return ref[0, 0, pl.ds(pl.multiple_of(j * TK, TK), TK), :]


def _group_rows(vals, shape, axis):
    g = lax.broadcasted_iota(I32, shape, axis) >> TQ_SHIFT
    out = jnp.full(shape, vals[GROUP - 1], F32)
    for i in range(GROUP - 2, -1, -1):
        out = jnp.where(g == i, vals[i], out)
    return out


def _kv_slopes(slopes_ref, kk, shape, axis):
    return _group_rows([slopes_ref[kk * GROUP + g] for g in range(GROUP)], shape, axis)


def _slope_rows(slope):
    row = lax.broadcasted_iota(I32, slope.shape, 0)
    out = jnp.zeros(slope.shape, F32)
    rem = slope
    for t in range(N_SLOPE_TERMS):
        piece = rem.astype(BF16).astype(F32)
        out = jnp.where(row == t, piece, out)
        rem = rem - piece
    return out.astype(BF16)


def _tile_coords():
    key = lax.broadcasted_iota(I32, (TK, GT), 0)
    qoff = lax.broadcasted_iota(I32, (TK, GT), 1) & (TQ - 1)
    return key, qoff


def _online_step(s, tile_shift, vj, m_sc, l_sc, acc_sc):
    m_prev = m_sc[...]
    m_new = jnp.maximum(m_prev, jnp.max(s, axis=0, keepdims=True) - tile_shift)
    alpha = jnp.exp(m_prev - m_new)
    p = jnp.exp(s - (m_new + tile_shift))
    l_sc[...] = alpha * l_sc[...] + jnp.sum(p, axis=0, keepdims=True)
    acc_sc[...] = acc_sc[...] * alpha + _dot(vj, p.astype(BF16))
    m_sc[...] = m_new


def _sb_tile(z, uo, cb, before):
    sp = jnp.maximum(z, 0.0) + jnp.log(1.0 + jnp.exp(-jnp.abs(z)))
    lm = -sp
    if before is not None:
        lm = jnp.where(before, lm, 0.0)
    hi, lo = _split_bf16(lm)
    r = _dot(hi, uo) + _dot(lo, uo)
    w = jnp.exp((z - sp) + r[:, :TK] + cb)
    if before is not None:
        w = jnp.where(before, w, 0.0)
    return w, cb + r[:, TK:]


def _sb_tile_t(z, ut, cb, before):
    sp = jnp.maximum(z, 0.0) + jnp.log(1.0 + jnp.exp(-jnp.abs(z)))
    lm = -sp
    if before is not None:
        lm = jnp.where(before, lm, 0.0)
    hi, lo = _split_bf16(lm)
    w = jnp.exp((z - sp) + (_dot(ut, hi) + _dot(ut, lo)) + cb)
    if before is not None:
        w = jnp.where(before, w, 0.0)
    return w, cb + jnp.sum(lm, axis=0, keepdims=True)


def _sb_prompt_kernel(qt_ref, k_ref, v_ref, ut_ref, o_ref, acc_sc, cb_sc):
    i = pl.program_id(2)
    qa = jnp.concatenate([_stack_queries(qt_ref), jnp.zeros((HEAD_DIM, GT), BF16)], axis=0)
    ut = ut_ref[...]
    acc_sc[...] = jnp.zeros_like(acc_sc)
    cb_sc[...] = jnp.zeros_like(cb_sc)

    def tile(j, diag):
        z = _dot(_key_rows(k_ref, j), qa)
        before = None
        if diag:
            key, qoff = _tile_coords()
            before = key < qoff
        w, cb = _sb_tile_t(z, ut, cb_sc[...], before)
        acc_sc[...] += _dot(_key_tile(v_ref, j), w.astype(BF16))
        cb_sc[...] = cb
        return jnp.max(cb, axis=1, keepdims=True)[0, 0]

    live0 = tile(i, True)

    def cond(c):
        return jnp.logical_and(c[0] >= 0, c[1] > SB_DEAD_LOG)

    def body(c):
        return c[0] - 1, tile(c[0], False)

    lax.while_loop(cond, body, (i - 1, live0))
    o_ref[0] = _untranspose(acc_sc[...]).astype(o_ref.dtype)


def _prompt_specs(s):
    qt_spec = pl.BlockSpec((1, KV_DIM, TQ), lambda bb, kk, ii, *_: (bb, kk, ii))
    k_spec = pl.BlockSpec((1, 1, s, 2 * HEAD_DIM), lambda bb, kk, ii, *_: (bb, kk, 0, 0))
    v_spec = pl.BlockSpec((1, 1, HEAD_DIM, s), lambda bb, kk, ii, *_: (bb, kk, 0, 0))
    o_spec = pl.BlockSpec((1, TQ, KV_DIM), lambda bb, kk, ii, *_: (bb, ii, kk))
    return qt_spec, k_spec, v_spec, o_spec


def _stat_scratch():
    return pltpu.VMEM((1, GT), F32)


def _acc_scratch():
    return pltpu.VMEM((HEAD_DIM, GT), F32)


def sb_prompt_attention(qt, k_rows, vt, ut):
    b, _, s = qt.shape
    qt_spec, k_spec, v_spec, o_spec = _prompt_specs(s)
    return pl.pallas_call(
        _sb_prompt_kernel,
        grid=(b, N_KV_HEADS, s // TQ),
        in_specs=[qt_spec, k_spec, v_spec, pl.BlockSpec((TK, TK), lambda bb, kk, ii: (0, 0))],
        out_specs=o_spec,
        out_shape=jax.ShapeDtypeStruct((b, s, Q_DIM), BF16),
        scratch_shapes=[_acc_scratch(), _stat_scratch()],
        compiler_params=_cparams(("parallel", "parallel", "arbitrary")),
        name="sb_prompt",
    )(qt, k_rows, vt, ut)


def _fold_heads(res):
    kv_of_row = lax.broadcasted_iota(I32, (N_HEADS, HEAD_DIM), 0) >> GROUP_SHIFT
    out = jnp.zeros((N_HEADS, HEAD_DIM), F32)
    for k in range(N_KV_HEADS):
        out = out + jnp.where(kv_of_row == k, res[:, k * HEAD_DIM:(k + 1) * HEAD_DIM], 0.0)
    return out


def _sb_sample_kernel(n_pages_step, pt_ref, dead_ref, q_ref, uo_ref, acc_in_ref, cb_in_ref, *rest):
    k_refs = rest[:n_pages_step]
    v_refs = rest[n_pages_step:2 * n_pages_step]
    o_ref, acc_ref, cb_ref, dead_sc = rest[2 * n_pages_step:]
    p = pl.program_id(1)

    @pl.when(p == 0)
    def _():
        acc_ref[0] = acc_in_ref[0]
        cb_ref[0] = cb_in_ref[0]
        dead_sc[0] = dead_ref[pl.program_id(0)]

    q = q_ref[0]
    uo = uo_ref[...]
    for i in range(n_pages_step):
        @pl.when(dead_sc[0] == 0)
        def _(i=i):
            z = _dot(q, k_refs[i][0].astype(BF16))
            w, cb = _sb_tile(z, uo, cb_ref[0], None)
            acc_ref[0] += _nt_dot(w.astype(BF16), v_refs[i][0].astype(BF16))
            cb_ref[0] = cb
            live = jnp.max(cb, axis=0, keepdims=True)[0, 0]
            dead_sc[0] = (live <= SB_DEAD_LOG).astype(I32)

    @pl.when(p == pl.num_programs(1) - 1)
    def _():
        o_ref[0] = _fold_heads(acc_ref[0])


def sb_sample_pass(qbd, pool_k, pool_v, page_base, page_table, dead, acc, cb, uo, first_page, n_steps,
                   n_pages_step):
    b = qbd.shape[0]
    page = pool_k.shape[2]

    def page_spec(i):
        def index(bb, pp, pt, dd):
            pg = pt[bb, first_page - (pp * n_pages_step + i)]
            return (page_base + pg * (1 - dd[bb]), 0, 0)
        return pl.BlockSpec((1, KV_DIM, page), index)

    pages = [page_spec(i) for i in range(n_pages_step)]
    per_b = lambda shape: pl.BlockSpec((1,) + shape, lambda bb, pp, pt, dd: (bb, 0, 0))
    grid_spec = pltpu.PrefetchScalarGridSpec(
        num_scalar_prefetch=2,
        grid=(b, n_steps),
        in_specs=[per_b((N_HEADS, KV_DIM)),
                  pl.BlockSpec((page, 2 * page), lambda bb, pp, pt, dd: (0, 0)),
                  per_b((N_HEADS, KV_DIM)), per_b((N_HEADS, page))] + pages + pages,
        out_specs=[per_b((N_HEADS, HEAD_DIM)), per_b((N_HEADS, KV_DIM)), per_b((N_HEADS, page))],
        scratch_shapes=[pltpu.SMEM((1,), I32)],
    )
    return pl.pallas_call(
        functools.partial(_sb_sample_kernel, n_pages_step),
        grid_spec=grid_spec,
        out_shape=[jax.ShapeDtypeStruct((b, N_HEADS, HEAD_DIM), F32),
                   jax.ShapeDtypeStruct((b, N_HEADS, KV_DIM), F32),
                   jax.ShapeDtypeStruct((b, N_HEADS, page), F32)],
        compiler_params=_cparams(("parallel", "arbitrary")),
        name="sb_sample",
    )(page_table, dead, qbd, uo, acc, cb, *([pool_k] * n_pages_step), *([pool_v] * n_pages_step))


def sb_sample_attention(qbd, pool_k, pool_v, page_base, page_table, uo):
    b = qbd.shape[0]
    n_pages = page_table.shape[1]
    page = pool_k.shape[2]
    head = min(SB_HEAD_PAGES, n_pages)
    acc = jnp.zeros((b, N_HEADS, KV_DIM), F32)
    cb = jnp.zeros((b, N_HEADS, page), F32)
    dead = jnp.zeros((b,), I32)
    o, acc, cb = sb_sample_pass(qbd, pool_k, pool_v, page_base, page_table, dead, acc, cb, uo,
                                n_pages - 1, 1, head)
    tail = n_pages - head
    if tail:
        per_step = _largest_divisor(tail, SB_TAIL_PAGES_PER_STEP)
        dead = (jnp.max(cb, axis=(1, 2)) <= SB_DEAD_LOG).astype(I32)
        o = lax.cond(jnp.all(dead != 0), lambda: o,
                     lambda: sb_sample_pass(qbd, pool_k, pool_v, page_base, page_table, dead, acc, cb, uo,
                                            tail - 1, tail // per_step, per_step)[0])
    return o


def _banded_kernel(window, has_sink, slopes_ref, sinks_ref, qt_ref, k_ref, v_ref, o_ref,
                   m_sc, l_sc, acc_sc):
    kk = pl.program_id(1)
    i = pl.program_id(2)
    slope1 = _kv_slopes(slopes_ref, kk, (1, GT), 1)
    qa = jnp.concatenate([_stack_queries(qt_ref),
                          _slope_rows(_kv_slopes(slopes_ref, kk, (HEAD_DIM, GT), 1))], axis=0)
    key, qoff = _tile_coords()
    if has_sink:
        sink = _group_rows([sinks_ref[kk * GROUP + g] for g in range(GROUP)], (1, GT), 1)
        qoff1 = lax.broadcasted_iota(I32, (1, GT), 1) & (TQ - 1)
        m_sc[...] = sink + slope1 * qoff1.astype(F32)
        l_sc[...] = jnp.ones_like(l_sc)
    else:
        m_sc[...] = jnp.full_like(m_sc, NEG_INF)
        l_sc[...] = jnp.zeros_like(l_sc)
    acc_sc[...] = jnp.zeros_like(acc_sc)
    n_back = window // TK
    for c in range(n_back + 1):
        jt = i - n_back + c

        @pl.when(jt >= 0)
        def _(jt=jt, c=c):
            s = _dot(_key_rows(k_ref, jt), qa)
            if c == 0:
                s = jnp.where(key >= qoff, s, NEG_INF)
            if c == n_back:
                s = jnp.where(key <= qoff, s, NEG_INF)
            _online_step(s, slope1 * float(TK * (n_back - c)), _key_tile(v_ref, jt), m_sc, l_sc, acc_sc)

    o_ref[0] = _untranspose(acc_sc[...] / l_sc[...]).astype(o_ref.dtype)


def banded_attention(qt, k_rows, vt, slopes, sinks, window, has_sink, out_dtype):
    b, _, s = qt.shape
    assert window % TK == 0
    qt_spec, k_spec, v_spec, o_spec = _prompt_specs(s)
    smem = pl.BlockSpec(memory_space=pltpu.SMEM)
    return pl.pallas_call(
        functools.partial(_banded_kernel, window, has_sink),
        grid=(b, N_KV_HEADS, s // TQ),
        in_specs=[smem, smem, qt_spec, k_spec, v_spec],
        out_specs=o_spec,
        out_shape=jax.ShapeDtypeStruct((b, s, Q_DIM), out_dtype),
        scratch_shapes=[_stat_scratch(), _stat_scratch(), _acc_scratch()],
        compiler_params=_cparams(("parallel", "parallel", "parallel")),
        name="banded_attention",
    )(slopes, sinks, qt, k_rows, vt)


def _nsa_selected_prompt_kernel(flags_ref, slopes_ref, qt_ref, selt_ref, et_ref, k_ref, v_ref, o_ref,
                                m_sc, l_sc, acc_sc):
    bb = pl.program_id(0)
    kk = pl.program_id(1)
    i = pl.program_id(2)
    nq = pl.num_programs(2)
    slope1 = _kv_slopes(slopes_ref, kk, (1, GT), 1)
    selt = selt_ref[0, 0].astype(F32)
    off = jnp.concatenate([(selt - 1.0) * (-NEG_INF)] * GROUP, axis=1).astype(BF16)
    qa = jnp.concatenate([_stack_queries(qt_ref),
                          _slope_rows(_kv_slopes(slopes_ref, kk, (HEAD_DIM, GT), 1)), off], axis=0)
    tile_step = slope1 * float(TK)
    m_sc[...] = jnp.full_like(m_sc, NEG_INF)
    l_sc[...] = jnp.zeros_like(l_sc)
    acc_sc[...] = jnp.zeros_like(acc_sc)
    tile_bits = flags_ref[(bb * N_KV_HEADS + kk) * nq + i]

    def scores(j):
        ka = jnp.concatenate([_key_rows(k_ref, j), et_ref[j]], axis=1)
        return _dot(ka, qa)

    def body(j, carry):
        @pl.when(((tile_bits >> j) & 1) != 0)
        def _():
            _online_step(scores(j), tile_step * (i - j).astype(F32), _key_tile(v_ref, j),
                         m_sc, l_sc, acc_sc)
        return carry

    lax.fori_loop(0, i, body, 0)
    key, qoff = _tile_coords()
    s = jnp.where(key <= qoff, scores(i), NEG_INF)
    _online_step(s, jnp.zeros((1, GT), F32), _key_tile(v_ref, i), m_sc, l_sc, acc_sc)
    o_ref[0] = _untranspose(acc_sc[...] / l_sc[...])


def nsa_selected_prompt(qt, sel_t, flags, expand_t, k_rows, vt, slopes):
    b, _, s = qt.shape
    assert s // TK <= 32
    nb = sel_t.shape[2]
    qt_spec, k_spec, v_spec, o_spec = _prompt_specs(s)
    grid_spec = pltpu.PrefetchScalarGridSpec(
        num_scalar_prefetch=1,
        grid=(b, N_KV_HEADS, s // TQ),
        in_specs=[pl.BlockSpec(memory_space=pltpu.SMEM), qt_spec,
                  pl.BlockSpec((1, 1, nb, TQ), lambda bb, kk, ii, fl: (bb, kk, 0, ii)),
                  pl.BlockSpec(expand_t.shape, lambda bb, kk, ii, fl: (0, 0, 0)),
                  k_spec, v_spec],
        out_specs=o_spec,
        scratch_shapes=[_stat_scratch(), _stat_scratch(), _acc_scratch()],
    )
    return pl.pallas_call(
        _nsa_selected_prompt_kernel,
        grid_spec=grid_spec,
        out_shape=jax.ShapeDtypeStruct((b, s, Q_DIM), F32),
        compiler_params=_cparams(("parallel", "parallel", "parallel")),
        name="nsa_selected_prompt",
    )(flags, slopes, qt, sel_t, expand_t, k_rows, vt)


def _compress_prompt_kernel(k_ref, v_ref, wk_ref, wv_ref, e_ref, et_ref, kc_ref, vct_ref):
    nb = e_ref.shape[1]
    n_chunks = e_ref.shape[0]
    kc = jnp.zeros((nb, KV_DIM), F32)
    vct = jnp.zeros((KV_DIM, nb), F32)
    for c in range(n_chunks):
        khi, klo = _split_bf16(k_ref[0, :, c * LANES:(c + 1) * LANES] * wk_ref[...])
        vhi, vlo = _split_bf16(v_ref[0, :, c * LANES:(c + 1) * LANES] * wv_ref[...])
        kc = kc + _nt_dot(e_ref[c], khi) + _nt_dot(e_ref[c], klo)
        vct = vct + _dot(vhi, et_ref[c]) + _dot(vlo, et_ref[c])
    for k in range(N_KV_HEADS):
        kc_ref[0, k] = kc[:, k * HEAD_DIM:(k + 1) * HEAD_DIM]
        vct_ref[0, k] = vct[k * HEAD_DIM:(k + 1) * HEAD_DIM, :]


def compress_prompt(kct_stream, vct_stream, wk_t, wv_t, expand, expand_t):
    b, _, s = kct_stream.shape
    nb = s // NSA_BLOCK
    x_spec = pl.BlockSpec((1, KV_DIM, s), lambda bb: (bb, 0, 0))
    w_spec = pl.BlockSpec((KV_DIM, LANES), lambda bb: (0, 0))
    full = lambda a: pl.BlockSpec(a.shape, lambda bb: (0,) * a.ndim)
    row_major = pl.BlockSpec((1, N_KV_HEADS, nb, HEAD_DIM), lambda bb: (bb, 0, 0, 0))
    col_major = pl.BlockSpec((1, N_KV_HEADS, HEAD_DIM, nb), lambda bb: (bb, 0, 0, 0))
    return pl.pallas_call(
        _compress_prompt_kernel,
        grid=(b,),
        in_specs=[x_spec, x_spec, w_spec, w_spec, full(expand), full(expand_t)],
        out_specs=[row_major, col_major],
        out_shape=[jax.ShapeDtypeStruct((b, N_KV_HEADS, nb, HEAD_DIM), F32),
                   jax.ShapeDtypeStruct((b, N_KV_HEADS, HEAD_DIM, nb), F32)],
        compiler_params=_cparams(("parallel",)),
        name="compress_prompt",
    )(kct_stream, vct_stream, wk_t, wv_t, expand, expand_t)


def _compress_sample_kernel(n_pages_step, pt_ref, wk_ref, wv_ref, *rest):
    k_refs = rest[:n_pages_step]
    v_refs = rest[n_pages_step:2 * n_pages_step]
    ok_ref, ov_ref = rest[2 * n_pages_step:]
    pp = pl.program_id(1)
    page = k_refs[0].shape[2]
    nb = ok_ref.shape[2]
    per_page = page // NSA_BLOCK

    @pl.when(pp == 0)
    def _():
        ok_ref[...] = jnp.zeros_like(ok_ref)
        ov_ref[...] = jnp.zeros_like(ov_ref)

    rows = n_pages_step * page
    blk_of_row = lax.broadcasted_iota(I32, (rows, nb), 0) >> NSA_BLOCK_SHIFT
    col = lax.broadcasted_iota(I32, (rows, nb), 1)
    place = jnp.where(col == pp * (n_pages_step * per_page) + blk_of_row, 1.0, 0.0).astype(BF16)
    kw = jnp.concatenate([(r[0] * wk_ref[...]).astype(BF16) for r in k_refs], axis=1)
    vw = jnp.concatenate([(r[0] * wv_ref[...]).astype(BF16) for r in v_refs], axis=1)
    sums = _dot(jnp.concatenate([kw, vw], axis=0), place)
    ok_ref[0] += sums[0:KV_DIM]
    ov_ref[0] += sums[KV_DIM:2 * KV_DIM]


def compress_sample(pool_k, pool_v, page_base, page_table, wk_t, wv_t):
    b, n_pages = page_table.shape
    page = pool_k.shape[2]
    nb = n_pages * page // NSA_BLOCK
    per_step = _largest_divisor(n_pages, COMPRESS_PAGES_PER_STEP)

    def page_spec(i):
        return pl.BlockSpec((1, KV_DIM, page),
                            lambda bb, pp, pt: (page_base + pt[bb, pp * per_step + i], 0, 0))

    pages = [page_spec(i) for i in range(per_step)]
    w_spec = pl.BlockSpec((KV_DIM, page), lambda bb, pp, pt: (0, 0))
    o_spec = pl.BlockSpec((1, KV_DIM, nb), lambda bb, pp, pt: (bb, 0, 0))
    shape = jax.ShapeDtypeStruct((b, KV_DIM, nb), F32)
    grid_spec = pltpu.PrefetchScalarGridSpec(
        num_scalar_prefetch=1,
        grid=(b, n_pages // per_step),
        in_specs=[w_spec, w_spec] + pages + pages,
        out_specs=[o_spec, o_spec],
    )
    return pl.pallas_call(
        functools.partial(_compress_sample_kernel, per_step),
        grid_spec=grid_spec,
        out_shape=[shape, shape],
        compiler_params=_cparams(("parallel", "arbitrary")),
        name="compress_sample",
    )(page_table, wk_t, wv_t, *([pool_k] * per_step), *([pool_v] * per_step))


def _rank_rows(score_ref):
    n = score_ref.shape[0]
    score = score_ref[...]
    idx = lax.broadcasted_iota(I32, score.shape, 0)
    rank = jnp.zeros(score.shape, F32)
    for m in range(n):
        row_m = score_ref[m:m + 1, :]
        beats = jnp.where(row_m > score, 1.0, jnp.where(row_m == score, jnp.where(idx > m, 1.0, 0.0), 0.0))
        rank = rank + beats
    return rank


def _nsa_cmp_prompt_kernel(slopes_ref, qt_ref, kc_ref, vct_ref, o_ref, selt_ref, score_sc):
    kk = pl.program_id(1)
    i = pl.program_id(2)
    nb = kc_ref.shape[2]
    st = _dot(kc_ref[0, 0].astype(BF16), _stack_queries(qt_ref))
    blk = lax.broadcasted_iota(I32, (nb, GT), 0)
    q_pos_t = i * TQ + (lax.broadcasted_iota(I32, (nb, GT), 1) & (TQ - 1))
    dist_t = q_pos_t - (blk * NSA_BLOCK + NSA_BLOCK - 1)
    valid_t = dist_t >= 0
    st = jnp.where(valid_t, st - _kv_slopes(slopes_ref, kk, (nb, GT), 1) * dist_t.astype(F32), NEG_INF)
    pt = jnp.where(valid_t, jnp.exp(st - jnp.max(st, axis=0, keepdims=True)), 0.0)
    pt = pt / jnp.maximum(jnp.sum(pt, axis=0, keepdims=True), TINY)
    o_ref[0] = _untranspose(_dot(vct_ref[0, 0].astype(BF16), pt.astype(BF16)))
    imp = pt[:, 0:TQ]
    for g in range(1, GROUP):
        imp = imp + pt[:, g * TQ:(g + 1) * TQ]
    blk_q = lax.broadcasted_iota(I32, (nb, TQ), 0)
    cur = (i * TQ + lax.broadcasted_iota(I32, (nb, TQ), 1)) >> NSA_BLOCK_SHIFT
    score_sc[...] = jnp.where(blk_q == cur, FORCE_SCORE, jnp.where(blk_q < cur, imp, -1.0))
    rank = _rank_rows(score_sc)
    selt_ref[0, 0] = jnp.where(rank < float(min(NSA_TOPK, nb)), 1.0, 0.0).astype(BF16)


def nsa_cmp_prompt(qt, kcmp, vcmp_t, slopes):
    b, _, s = qt.shape
    nb = kcmp.shape[2]
    qt_spec, _, _, o_spec = _prompt_specs(s)
    row_major = pl.BlockSpec((1, 1, nb, HEAD_DIM), lambda bb, kk, ii: (bb, kk, 0, 0))
    col_major = pl.BlockSpec((1, 1, HEAD_DIM, nb), lambda bb, kk, ii: (bb, kk, 0, 0))
    return pl.pallas_call(
        _nsa_cmp_prompt_kernel,
        grid=(b, N_KV_HEADS, s // TQ),
        in_specs=[pl.BlockSpec(memory_space=pltpu.SMEM), qt_spec, row_major, col_major],
        out_specs=[o_spec, pl.BlockSpec((1, 1, nb, TQ), lambda bb, kk, ii: (bb, kk, 0, ii))],
        out_shape=[jax.ShapeDtypeStruct((b, s, Q_DIM), F32),
                   jax.ShapeDtypeStruct((b, N_KV_HEADS, nb, s), BF16)],
        scratch_shapes=[pltpu.VMEM((nb, TQ), F32)],
        compiler_params=_cparams(("parallel", "parallel", "parallel")),
        name="nsa_cmp_prompt",
    )(slopes, qt, kcmp, vcmp_t)


def _nsa_cmp_sample_kernel(past_len, q_ref, slope_ref, gsum_ref, kc_ref, vc_ref, o_ref, idx_ref):
    nb = kc_ref.shape[2]
    s = _dot(q_ref[0], kc_ref[0].astype(BF16))
    blk = lax.broadcasted_iota(I32, (N_HEADS, nb), 1)
    dist = past_len - (blk * NSA_BLOCK + NSA_BLOCK - 1)
    valid = dist >= 0
    s = jnp.where(valid, s - slope_ref[...][:, :1] * dist.astype(F32), NEG_INF)
    m = jnp.max(s, axis=1, keepdims=True)
    p = jnp.where(valid, jnp.exp(s - m), 0.0)
    p = p / jnp.maximum(jnp.sum(p, axis=1, keepdims=True), TINY)
    o_ref[0] = _fold_heads(_nt_dot(p.astype(BF16), vc_ref[0].astype(BF16)))
    imp = jnp.dot(gsum_ref[...], p, preferred_element_type=F32, precision=lax.Precision.HIGHEST)
    rows = imp.shape[0]
    blk8 = lax.broadcasted_iota(I32, (rows, nb), 1)
    rank = jnp.zeros((rows, nb), F32)
    for mcol in range(nb):
        col_m = imp[:, mcol:mcol + 1]
        rank = rank + jnp.where(col_m > imp, 1.0,
                                jnp.where(col_m == imp, jnp.where(blk8 > mcol, 1.0, 0.0), 0.0))
    blk_f = blk8.astype(F32)
    out = jnp.zeros((rows, nb), F32)
    for r in range(NSA_TOPK - 1):
        idx_r = jnp.sum(jnp.where(rank == float(r), blk_f, 0.0), axis=1, keepdims=True)
        out = jnp.where(blk8 == r, idx_r, out)
    idx_ref[0] = out.astype(I32)


def nsa_cmp_sample(qbd, slope_rows, gsum, kcmp_t, vcmp_t, past_len):
    b = qbd.shape[0]
    nb = kcmp_t.shape[2]
    rows = gsum.shape[0]
    return pl.pallas_call(
        functools.partial(_nsa_cmp_sample_kernel, past_len),
        grid=(b,),
        in_specs=[pl.BlockSpec((1, N_HEADS, KV_DIM), lambda bb: (bb, 0, 0)),
                  pl.BlockSpec((N_HEADS, LANES), lambda bb: (0, 0)),
                  pl.BlockSpec((rows, N_HEADS), lambda bb: (0, 0)),
                  pl.BlockSpec((1, KV_DIM, nb), lambda bb: (bb, 0, 0)),
                  pl.BlockSpec((1, KV_DIM, nb), lambda bb: (bb, 0, 0))],
        out_specs=[pl.BlockSpec((1, N_HEADS, HEAD_DIM), lambda bb: (bb, 0, 0)),
                   pl.BlockSpec((1, rows, nb), lambda bb: (bb, 0, 0))],
        out_shape=[jax.ShapeDtypeStruct((b, N_HEADS, HEAD_DIM), F32),
                   jax.ShapeDtypeStruct((b, rows, nb), I32)],
        compiler_params=_cparams(("parallel",)),
        name="nsa_cmp_sample",
    )(qbd, slope_rows, gsum, kcmp_t, vcmp_t)


def _nsa_selected_sample_kernel(past_len, phys_ref, logi_ref, q_ref, slope_ref, nk_ref, nv_ref, *rest):
    n_past = NSA_TOPK - 1
    k_refs = rest[:n_past]
    v_refs = rest[n_past:2 * n_past]
    o_ref = rest[2 * n_past]
    page = k_refs[0].shape[2]
    per_page = page // NSA_BLOCK
    base = (pl.program_id(0) * N_KV_HEADS + pl.program_id(1)) * n_past
    q = q_ref[0, 0]
    slope = slope_ref[0]
    lane = lax.broadcasted_iota(I32, (N_HEADS, page), 1)
    scores, values = [], []
    for j in range(n_past):
        blk = logi_ref[base + j]
        first = (blk // per_page) * page
        dist = past_len - (first + lane)
        valid = jnp.logical_and((lane >> NSA_BLOCK_SHIFT) == blk % per_page, dist >= 0)
        z = _dot(q, k_refs[j][0].astype(BF16))
        scores.append(jnp.where(valid, z - slope[:, :1] * dist.astype(F32), NEG_INF))
        values.append(v_refs[j][0].astype(BF16))
    n_new = nk_ref.shape[2]
    is_new = lax.broadcasted_iota(I32, (N_HEADS, n_new), 1) == 0
    s_new = jnp.where(is_new, _nt_dot(q, nk_ref[0, 0].astype(BF16)), NEG_INF)
    m = jnp.max(s_new, axis=1, keepdims=True)
    for sc in scores:
        m = jnp.maximum(m, jnp.max(sc, axis=1, keepdims=True))
    p_new = jnp.exp(s_new - m)
    denom = jnp.sum(p_new, axis=1, keepdims=True)
    acc = _dot(p_new.astype(BF16), nv_ref[0, 0].astype(BF16))
    for sc, val in zip(scores, values):
        p = jnp.exp(sc - m)
        denom = denom + jnp.sum(p, axis=1, keepdims=True)
        acc = acc + _nt_dot(p.astype(BF16), val)
    o_ref[0, 0] = acc / denom


def nsa_selected_sample(q4, slope4, phys_idx, logi_idx, pool_k, pool_v, new_k, new_v, past_len):
    b = q4.shape[0]
    n_past = NSA_TOPK - 1
    page = pool_k.shape[2]

    def blk_spec(j):
        return pl.BlockSpec(
            (1, HEAD_DIM, page),
            lambda bb, kk, phys, logi: (phys[(bb * N_KV_HEADS + kk) * n_past + j], 0, 0))

    blks = [blk_spec(j) for j in range(n_past)]
    per_head = lambda shape: pl.BlockSpec((1, 1) + shape, lambda bb, kk, phys, logi: (bb, kk, 0, 0))
    grid_spec = pltpu.PrefetchScalarGridSpec(
        num_scalar_prefetch=2,
        grid=(b, N_KV_HEADS),
        in_specs=[per_head((N_HEADS, HEAD_DIM)),
                  pl.BlockSpec((1, N_HEADS, LANES), lambda bb, kk, phys, logi: (kk, 0, 0)),
                  per_head((NEW_ROWS, HEAD_DIM)), per_head((NEW_ROWS, HEAD_DIM))] + blks + blks,
        out_specs=per_head((N_HEADS, HEAD_DIM)),
    )
    return pl.pallas_call(
        functools.partial(_nsa_selected_sample_kernel, past_len),
        grid_spec=grid_spec,
        out_shape=jax.ShapeDtypeStruct((b, N_KV_HEADS, N_HEADS, HEAD_DIM), F32),
        compiler_params=_cparams(("parallel", "parallel")),
        name="nsa_selected_sample",
    )(phys_idx, logi_idx, q4, slope4, new_k, new_v, *([pool_k] * n_past), *([pool_v] * n_past))


def _window_sample_kernel(window, has_sink, q_ref, slope_ref, sink_ref, wk_ref, wv_ref,
                          nk_ref, nv_ref, o_ref):
    n_buf = wk_ref.shape[2]
    n_new = nk_ref.shape[1]
    q = q_ref[0]
    dist1 = n_buf - lax.broadcasted_iota(I32, (N_HEADS, n_buf), 1)
    valid1 = dist1 <= window
    s1 = jnp.where(valid1,
                   _dot(q, wk_ref[0].astype(BF16)) - slope_ref[...][:, :1] * dist1.astype(F32),
                   NEG_INF)
    valid2 = lax.broadcasted_iota(I32, (N_HEADS, n_new), 1) == 0
    s2 = jnp.where(valid2, _nt_dot(q, nk_ref[0].astype(BF16)), NEG_INF)
    m = jnp.maximum(jnp.max(s1, axis=1, keepdims=True), jnp.max(s2, axis=1, keepdims=True))
    if has_sink:
        sink = sink_ref[...][:, :1]
        m = jnp.maximum(m, sink)
    p1 = jnp.where(valid1, jnp.exp(s1 - m), 0.0)
    p2 = jnp.where(valid2, jnp.exp(s2 - m), 0.0)
    denom = jnp.sum(p1, axis=1, keepdims=True) + jnp.sum(p2, axis=1, keepdims=True)
    if has_sink:
        denom = denom + jnp.exp(sink - m)
    acc = _nt_dot(p1.astype(BF16), wv_ref[0].astype(BF16)) + _dot(p2.astype(BF16), nv_ref[0].astype(BF16))
    o_ref[0] = _fold_heads(acc / denom)


def window_sample_attention(qbd, slope_rows, sink_rows, win_kt, win_vt, new_k, new_v, window, has_sink):
    b, _, n_buf = win_kt.shape
    n_new = new_k.shape[1]
    const = pl.BlockSpec((N_HEADS, LANES), lambda bb: (0, 0))
    return pl.pallas_call(
        functools.partial(_window_sample_kernel, window, has_sink),
        grid=(b,),
        in_specs=[pl.BlockSpec((1, N_HEADS, KV_DIM), lambda bb: (bb, 0, 0)), const, const,
                  pl.BlockSpec((1, KV_DIM, n_buf), lambda bb: (bb, 0, 0)),
                  pl.BlockSpec((1, KV_DIM, n_buf), lambda bb: (bb, 0, 0)),
                  pl.BlockSpec((1, n_new, KV_DIM), lambda bb: (bb, 0, 0)),
                  pl.BlockSpec((1, n_new, KV_DIM), lambda bb: (bb, 0, 0))],
        out_specs=pl.BlockSpec((1, N_HEADS, HEAD_DIM), lambda bb: (bb, 0, 0)),
        out_shape=jax.ShapeDtypeStruct((b, N_HEADS, HEAD_DIM), F32),
        compiler_params=_cparams(("parallel",)),
        name="window_sample",
    )(qbd, slope_rows, sink_rows, win_kt, win_vt, new_k, new_v)


def _block_diag_queries(q):
    b = q.shape[0]
    qh = q.reshape(b, N_KV_HEADS, GROUP, 1, HEAD_DIM)
    eye = jnp.eye(N_KV_HEADS, dtype=q.dtype).reshape(N_KV_HEADS, 1, N_KV_HEADS, 1)
    return (qh * eye).reshape(b, N_HEADS, KV_DIM)


def _suffix_matrix(n):
    r = jnp.arange(n)
    strict = (r[:, None] > r[None, :]).astype(BF16)
    return jnp.concatenate([strict, jnp.ones((n, n), BF16)], axis=1)


def _new_token_rows(x):
    return jnp.pad(x[:, None, :], ((0, 0), (0, NEW_ROWS - 1), (0, 0)))


def _paged_view(pool):
    n_layers, n_pool, page = pool.shape[:3]
    return pool.transpose(0, 1, 3, 4, 2).reshape(n_layers * n_pool, KV_DIM, page)


def _window_view(cache):
    b, n = cache.shape[:2]
    return cache.transpose(0, 2, 3, 1).reshape(b, KV_DIM, n)


def _stream_leaf(t):
    b, _, n = t.shape
    return t.reshape(b, N_KV_HEADS, HEAD_DIM, n).transpose(0, 3, 1, 2)


def _position_weights(w, reps):
    return jnp.tile(jnp.repeat(w.T, HEAD_DIM, axis=0), (1, reps))


def kernel(x_prompt, x_sample, cache_sb_k, cache_sb_v, cache_nsa_cmp_k, cache_nsa_cmp_v,
           cache_nsa_sel_k, cache_nsa_sel_v, cache_nsa_win_k, cache_nsa_win_v, cache_swa_k,
           cache_swa_v, page_table, norm_mix, norm_ffn, norm_final, w_sb_qkv, w_sb_o, w_nsa_in,
           w_nsa_o, nsa_cmp_wk, nsa_cmp_wv, w_swa_qkv, w_swa_o, swa_sinks, w_ffn_up, w_ffn_down):
    b, s, d = x_prompt.shape
    db = x_sample.shape[0]
    depth = norm_mix.shape[0]
    n_pool, page = cache_sb_k.shape[1:3]
    n_pages = page_table.shape[1]
    past_len = n_pages * page
    nsa_buf = cache_nsa_win_k.shape[2]
    swa_buf = cache_swa_k.shape[2]
    nq = s // TQ

    heads = jnp.arange(1, N_HEADS + 1, dtype=F32)
    slopes = jnp.exp2(-8.0 * heads / N_HEADS)
    slope_rows = jnp.broadcast_to(slopes[:, None], (N_HEADS, LANES))
    uo = _suffix_matrix(TK)
    no_sinks = jnp.zeros((N_HEADS,), F32)

    xp = x_prompt.reshape(b * s, d)
    xs = x_sample.reshape(db, d)
    tm_p, tm_s = 256, db

    def split_weights(w, n_t, key_streams):
        wqt = w[:, :Q_DIM].T.astype(BF16)
        wt = w[:, Q_DIM:Q_DIM + n_t * KV_DIM].T.astype(BF16)
        wa = jnp.concatenate([w[:, Q_DIM + j * KV_DIM:Q_DIM + (j + 1) * KV_DIM] for j in key_streams],
                             axis=1).astype(BF16)
        return wqt, wt, wa

    def project(x_p, x_s, g, wqt, wt, wa, wg):
        outs_p = norm_proj(x_p, g, wqt, wt, wa, wg, b, tm_p)
        outs_s = norm_proj(x_s, g, wqt, wt, None, wg, 1, tm_s)
        return outs_p, outs_s

    def sample_rows(t):
        return t[0].T

    heads_view = lambda t: t.reshape(b, N_KV_HEADS, HEAD_DIM, s)
    ut = _suffix_matrix(TK)[:, :TK].T

    outs = {name: [] for name in (
        "sb_k_p", "sb_v_p", "sb_k_s", "sb_v_s", "cmp_k_p", "cmp_v_p", "cmp_k_s", "cmp_v_s",
        "sel_k_p", "sel_v_p", "sel_k_s", "sel_v_s", "win_k_p", "win_v_p", "win_k_s", "win_v_s",
        "swa_k_p", "swa_v_p", "swa_k_s", "swa_v_s")}

    def new_leaf(rows):
        return rows.reshape(db, 1, N_KV_HEADS, HEAD_DIM)

    def shifted_window(win_t, new_rows):
        return _stream_leaf(jnp.concatenate([win_t[:, :, 1:], new_rows[:, :, None]], axis=2))

    ia = ib = ic = 0
    for i in range(depth):
        kind = i % N_MIXERS
        if kind == 0:
            wqt, wt, wa = split_weights(w_sb_qkv[ia], 2, (0,))
            w_o = w_sb_o[ia].astype(BF16)
            (qt_p, kt, vt, k_rows), (qt_s, kt_s, vt_s) = project(xp, xs, norm_mix[i], wqt, wt, wa, None)
            o_p = sb_prompt_attention(qt_p, k_rows, heads_view(vt), ut)
            o_s = sb_sample_attention(_block_diag_queries(sample_rows(qt_s)), _paged_view(cache_sb_k),
                                      _paged_view(cache_sb_v), ia * n_pool, page_table, uo)
            xp = out_proj(xp, o_p.reshape(b * s, Q_DIM), w_o, tm_p)
            xs = out_proj(xs, o_s.reshape(db, Q_DIM), w_o, tm_s)
            outs["sb_k_p"].append(_stream_leaf(kt))
            outs["sb_v_p"].append(_stream_leaf(vt))
            outs["sb_k_s"].append(new_leaf(sample_rows(kt_s)))
            outs["sb_v_s"].append(new_leaf(sample_rows(vt_s)))
            ia += 1
        elif kind == 1:
            wqt, wt, wa = split_weights(w_nsa_in[ib], 6, (2, 4))
            n_gate = NSA_GATES * N_HEADS
            wg = jnp.pad(w_nsa_in[ib][:, Q_DIM + 6 * KV_DIM:], ((0, 0), (0, LANES - n_gate))).astype(BF16)
            w_o = w_nsa_o[ib].astype(BF16)
            gate_expand = (jnp.arange(LANES)[None, :, None]
                           == (jnp.arange(Q_DIM)[None, None, :] // HEAD_DIM) * NSA_GATES
                           + jnp.arange(NSA_GATES)[:, None, None]).astype(BF16)
            outs_p, outs_s = project(xp, xs, norm_mix[i], wqt, wt, wa, wg)

            qt_p, kc, vc, ks, vs, kw, vw, ks_rows, kw_rows, gl_p = outs_p
            nb = s // NSA_BLOCK
            per_tile = TK // NSA_BLOCK
            expand = (jnp.arange(nb)[None, :, None]
                      == (jnp.arange(s // TK)[:, None, None] * per_tile
                          + jnp.arange(TK)[None, None, :] // NSA_BLOCK)).astype(BF16)
            expand_t = expand.swapaxes(1, 2)
            kcmp, vcmp_t = compress_prompt(kc, vc, _position_weights(nsa_cmp_wk[ib], LANES // NSA_BLOCK),
                                           _position_weights(nsa_cmp_wv[ib], LANES // NSA_BLOCK),
                                           expand, expand_t)
            o_cmp, sel_t = nsa_cmp_prompt(qt_p, kcmp, vcmp_t, slopes)
            flags = sel_t.reshape(b, N_KV_HEADS, s // TK, per_tile, nq, TQ).max(axis=(3, 5))
            flags = jnp.sum((flags > 0).astype(I32) << jnp.arange(s // TK, dtype=I32)[:, None], axis=2).reshape(-1)
            o_sel = nsa_selected_prompt(qt_p, sel_t, flags, expand_t, ks_rows, heads_view(vs), slopes)
            o_win = banded_attention(qt_p, kw_rows, heads_view(vw), slopes, no_sinks, NSA_WINDOW, False, F32)
            xp = nsa_out_proj(xp, gl_p, gate_expand,
                              [o.reshape(b * s, Q_DIM) for o in (o_cmp, o_sel, o_win)], w_o, tm_p)
            for name, val in (("cmp_k_p", kc), ("cmp_v_p", vc), ("sel_k_p", ks), ("sel_v_p", vs)):
                outs[name].append(_stream_leaf(val))
            outs["win_k_p"].append(_stream_leaf(kw[:, :, s - nsa_buf:]))
            outs["win_v_p"].append(_stream_leaf(vw[:, :, s - nsa_buf:]))

            qt_s, kc, vc, ks, vs, kw, vw, gl_s = outs_s
            q_s, kc, vc, ks, vs, kw, vw = [sample_rows(t) for t in (qt_s, kc, vc, ks, vs, kw, vw)]
            qbd = _block_diag_queries(q_s)
            kcmp_t, vcmp_t = compress_sample(_paged_view(cache_nsa_cmp_k), _paged_view(cache_nsa_cmp_v),
                                             ib * n_pool, page_table,
                                             _position_weights(nsa_cmp_wk[ib], page // NSA_BLOCK),
                                             _position_weights(nsa_cmp_wv[ib], page // NSA_BLOCK))
            gsum = (jnp.arange(8)[:, None] == jnp.arange(N_HEADS)[None, :] // GROUP).astype(F32)
            o_cmp, idx = nsa_cmp_sample(qbd, slope_rows, gsum, kcmp_t, vcmp_t, past_len)
            idx = idx[:, :N_KV_HEADS, :NSA_TOPK - 1]
            per_page = page // NSA_BLOCK
            pages = jnp.take_along_axis(page_table[:, None, :], idx // per_page, axis=2)
            phys = ((ib * n_pool + pages) * N_KV_HEADS + jnp.arange(N_KV_HEADS)[None, :, None])
            head_pages = lambda pool: _paged_view(pool).reshape(-1, HEAD_DIM, page)
            q4 = jnp.pad(q_s.reshape(db, N_KV_HEADS, GROUP, HEAD_DIM),
                         ((0, 0), (0, 0), (0, N_HEADS - GROUP), (0, 0)))
            slope4 = jnp.pad(jnp.broadcast_to(slopes.reshape(N_KV_HEADS, GROUP, 1),
                                              (N_KV_HEADS, GROUP, LANES)),
                             ((0, 0), (0, N_HEADS - GROUP), (0, 0)))
            new_head_rows = lambda x: jnp.pad(x.reshape(db, N_KV_HEADS, 1, HEAD_DIM),
                                              ((0, 0), (0, 0), (0, NEW_ROWS - 1), (0, 0)))
            o_sel = nsa_selected_sample(q4, slope4, phys.reshape(-1).astype(I32), idx.reshape(-1),
                                        head_pages(cache_nsa_sel_k), head_pages(cache_nsa_sel_v),
                                        new_head_rows(ks), new_head_rows(vs), past_len)
            o_sel = o_sel[:, :, :GROUP]
            win_kt = _window_view(cache_nsa_win_k[ib])
            win_vt = _window_view(cache_nsa_win_v[ib])
            o_win = window_sample_attention(qbd, slope_rows, slope_rows, win_kt, win_vt,
                                            _new_token_rows(kw), _new_token_rows(vw), NSA_WINDOW, False)
            xs = nsa_out_proj(xs, gl_s, gate_expand,
                              [o.reshape(db, Q_DIM) for o in (o_cmp, o_sel, o_win)], w_o, tm_s)
            for name, val in (("cmp_k_s", kc), ("cmp_v_s", vc), ("sel_k_s", ks), ("sel_v_s", vs)):
                outs[name].append(new_leaf(val))
            outs["win_k_s"].append(shifted_window(win_kt, kw))
            outs["win_v_s"].append(shifted_window(win_vt, vw))
            ib += 1
        else:
            wqt, wt, wa = split_weights(w_swa_qkv[ic], 2, (0,))
            w_o = w_swa_o[ic].astype(BF16)
            sinks = swa_sinks[ic]
            (qt_p, kt, vt, k_rows), (qt_s, kt_s, vt_s) = project(xp, xs, norm_mix[i], wqt, wt, wa, None)
            o_p = banded_attention(qt_p, k_rows, heads_view(vt), slopes, sinks, SWA_WINDOW, True, BF16)
            q_s, k_new, v_new = sample_rows(qt_s), sample_rows(kt_s), sample_rows(vt_s)
            buf_kt = _window_view(cache_swa_k[ic])
            buf_vt = _window_view(cache_swa_v[ic])
            sink_rows = jnp.broadcast_to(sinks[:, None], (N_HEADS, LANES))
            o_s = window_sample_attention(_block_diag_queries(q_s), slope_rows, sink_rows, buf_kt, buf_vt,
                                          _new_token_rows(k_new), _new_token_rows(v_new), SWA_WINDOW, True)
            xp = out_proj(xp, o_p.reshape(b * s, Q_DIM), w_o, tm_p)
            xs = out_proj(xs, o_s.reshape(db, Q_DIM), w_o, tm_s)
            outs["swa_k_p"].append(_stream_leaf(kt[:, :, s - swa_buf:]))
            outs["swa_v_p"].append(_stream_leaf(vt[:, :, s - swa_buf:]))
            outs["swa_k_s"].append(shifted_window(buf_kt, k_new))
            outs["swa_v_s"].append(shifted_window(buf_vt, v_new))
            ic += 1
        w_up = w_ffn_up[i].astype(BF16)
        w_down = w_ffn_down[i].astype(BF16)
        xp = ffn(xp, norm_ffn[i], w_up, w_down, 512, 512)
        xs = ffn(xs, norm_ffn[i], w_up, w_down, tm_s, 512)

    y_prompt = final_norm(xp, norm_final, tm_p).reshape(b, s, d)
    y_sample = final_norm(xs, norm_final, tm_s).reshape(db, 1, d)
    st = {name: jnp.stack(vals) for name, vals in outs.items()}
    return (y_prompt, y_sample,
            st["sb_k_p"], st["sb_v_p"], st["sb_k_s"], st["sb_v_s"],
            st["cmp_k_p"], st["cmp_v_p"], st["cmp_k_s"], st["cmp_v_s"],
            st["sel_k_p"], st["sel_v_p"], st["sel_k_s"], st["sel_v_s"],
            st["win_k_p"], st["win_v_p"], st["win_k_s"], st["win_v_s"],
            st["swa_k_p"], st["swa_v_p"], st["swa_k_s"], st["swa_v_s"])
```
